```python
import math
import jax, jax.numpy as jnp
from jax import lax
import numpy as np

D_MODEL = 4096
BATCH = 4
SEQ = 2048
DEPTH = 2
DEC_BATCH = 8
DEC_SEQ = 1
PAST_LEN = 16384
PAGE_SIZE = 128

N_MIXERS = 2
N_ATT = (DEPTH + 1) // 2
N_SSM = DEPTH // 2
HEAD_DIM = 128
N_HEADS = D_MODEL // HEAD_DIM
N_KV = 4
GQA = N_HEADS // N_KV
CMP_STRIDE = 16
CMP_LEN = 2 * CMP_STRIDE
CMP_HID = 2 * HEAD_DIM
SEL_BLOCK = 64
N_SELECT = 16
WINDOW = 512
Q_BLK = 64
ROPE_THETA = 10000.0
Q_WIDTH = N_HEADS * HEAD_DIM
KV_WIDTH = 6 * N_KV * HEAD_DIM
N_IN = Q_WIDTH + KV_WIDTH + 3 * N_HEADS
S5_GROUP = 16
S5_GROUPS = D_MODEL // S5_GROUP
S5_STATE = 64
D_FF = ((8 * D_MODEL + 3 * 256 - 1) // (3 * 256)) * 256
EPS = 1e-6
NEG_INF = -1e30
ATT_SCALE = HEAD_DIM ** -0.5

kernel_name = 'nsa_s5_hybrid_decode_step'


def rms_norm(x, g):
    xf = x.astype(jnp.float32)
    y = xf * lax.rsqrt(jnp.mean(xf * xf, axis=-1, keepdims=True) + EPS)
    return (y * g.astype(jnp.float32)).astype(x.dtype)


def rope(x, pos):
    half = HEAD_DIM // 2
    inv = ROPE_THETA ** (-jnp.arange(half, dtype=jnp.float32) / half)
    ang = pos.astype(jnp.float32)[:, None] * inv[None, :]
    cos = jnp.cos(ang)[None, :, None, :]
    sin = jnp.sin(ang)[None, :, None, :]
    xf = x.astype(jnp.float32)
    x1, x2 = xf[..., :half], xf[..., half:]
    return jnp.concatenate([x1 * cos - x2 * sin, x2 * cos + x1 * sin], axis=-1).astype(x.dtype)


def masked_softmax(s, mask):
    p = jax.nn.softmax(jnp.where(mask, s, NEG_INF), axis=-1)
    return jnp.where(mask, p, 0.0)


def swiglu(h, w_gu, w_down):
    g, u = jnp.split(h @ w_gu, 2, axis=-1)
    return (jax.nn.silu(g) * u) @ w_down


def nsa_project(h, pos, w_in):
    B, T, _ = h.shape
    proj = h @ w_in
    q = rope(proj[..., :Q_WIDTH].reshape(B, T, N_HEADS, HEAD_DIM), pos)
    kvr = proj[..., Q_WIDTH:Q_WIDTH + KV_WIDTH].reshape(B, T, 3, 2, N_KV, HEAD_DIM)
    k = rope(kvr[:, :, :, 0].reshape(B, T, 3 * N_KV, HEAD_DIM), pos).reshape(B, T, 3, N_KV, HEAD_DIM)
    kv = jnp.stack([k, kvr[:, :, :, 1]], axis=3).reshape(B, T, 6, N_KV, HEAD_DIM)
    gates = jax.nn.sigmoid(proj[..., Q_WIDTH + KV_WIDTH:].astype(jnp.float32)).reshape(B, T, N_HEADS, 3)
    return q, kv, gates


def compress(x, pe, w1, w2):
    B, Tk = x.shape[:2]
    n_chunk = Tk // CMP_STRIDE
    c = x[:, :n_chunk * CMP_STRIDE].reshape(B, n_chunk, CMP_STRIDE, N_KV, HEAD_DIM)
    blk = jnp.concatenate([c[:, :-1], c[:, 1:]], axis=2) + pe[:, None, :]
    flat = blk.transpose(0, 1, 3, 2, 4).reshape(B, n_chunk - 1, N_KV, CMP_LEN * HEAD_DIM)
    return jax.nn.gelu(flat @ w1) @ w2


def sel_blocks(x):
    B, Tk = x.shape[:2]
    n_sel = -(-Tk // SEL_BLOCK)
    x = jnp.pad(x, ((0, 0), (0, n_sel * SEL_BLOCK - Tk), (0, 0), (0, 0)))
    return x.reshape(B, n_sel, SEL_BLOCK, N_KV, HEAD_DIM).transpose(0, 3, 1, 2, 4)


def gather_blocks(blocks, idx):
    return jax.vmap(jax.vmap(lambda b, i: b[i]))(blocks, idx)


def nsa_attend(q, gates, pos_q, kc, vc, ks, vs, kw, vw, pos_w):
    f32 = jnp.float32
    B, Q = q.shape[:2]
    n_cmp = kc.shape[1]
    n_sel = ks.shape[2]
    qg = q.reshape(B, Q, N_KV, GQA, HEAD_DIM)
    blk_end = jnp.arange(n_cmp, dtype=jnp.int32) * CMP_STRIDE + CMP_LEN - 1
    m_cmp = blk_end[None, :] <= pos_q[:, None]
    s = jnp.einsum('bqkgd,bnkd->bkgqn', qg, kc, preferred_element_type=f32) * ATT_SCALE
    p_cmp = masked_softmax(s, m_cmp)
    o_cmp = jnp.einsum('bkgqn,bnkd->bqkgd', p_cmp.astype(vc.dtype), vc)
    ic = jnp.arange(n_cmp, dtype=jnp.int32)[:, None] * CMP_STRIDE
    js = jnp.arange(n_sel, dtype=jnp.int32)[None, :] * SEL_BLOCK
    overlap = ((ic < js + SEL_BLOCK) & (ic + CMP_LEN > js)).astype(f32)
    imp = jnp.einsum('bkgqn,ns->bkqs', p_cmp, overlap)
    jj = jnp.arange(n_sel, dtype=jnp.int32)[None, :]
    cur = (pos_q // SEL_BLOCK)[:, None]
    forced = (jj == 0) | (jj == cur) | (jj == cur - 1)
    future = jj * SEL_BLOCK > pos_q[:, None]
    score = jnp.where(future, -jnp.inf, jnp.where(forced, jnp.inf, imp))
    k_sel = min(N_SELECT, n_sel)
    _, idx = lax.top_k(score, k_sel)
    kg = gather_blocks(ks, idx)
    vg = gather_blocks(vs, idx)
    kpos = idx[..., None] * SEL_BLOCK + jnp.arange(SEL_BLOCK, dtype=jnp.int32)
    m_sel = (kpos <= pos_q[None, None, :, None, None]).reshape(B, N_KV, 1, Q, k_sel * SEL_BLOCK)
    s = jnp.einsum('bqkgd,bkqnld->bkgqnl', qg, kg, preferred_element_type=f32) * ATT_SCALE
    p_sel = masked_softmax(s.reshape(B, N_KV, GQA, Q, k_sel * SEL_BLOCK), m_sel)
    p_sel = p_sel.reshape(B, N_KV, GQA, Q, k_sel, SEL_BLOCK)
    o_sel = jnp.einsum('bkgqnl,bkqnld->bqkgd', p_sel.astype(vg.dtype), vg)
    dist = pos_q[:, None] - pos_w[None, :]
    m_win = (dist >= 0) & (dist <= WINDOW) & (pos_w[None, :] >= 0)
    s = jnp.einsum('bqkgd,bwkd->bkgqw', qg, kw, preferred_element_type=f32) * ATT_SCALE
    p_win = masked_softmax(s, m_win)
    o_win = jnp.einsum('bkgqw,bwkd->bqkgd', p_win.astype(vw.dtype), vw)
    g = gates.reshape(B, Q, N_KV, GQA, 3, 1)
    o = g[..., 0, :] * o_cmp + g[..., 1, :] * o_sel + g[..., 2, :] * o_win
    return o.reshape(B, Q, Q_WIDTH).astype(q.dtype)


def nsa_prompt(h, w_in, w_out, pe, w1, w2):
    B, T, _ = h.shape
    pos = jnp.arange(T, dtype=jnp.int32)
    q, kv, gates = nsa_project(h, pos, w_in)
    kc = compress(kv[:, :, 0], pe[0], w1[0], w2[0])
    vc = compress(kv[:, :, 1], pe[1], w1[1], w2[1])
    ks = sel_blocks(kv[:, :, 2])
    vs = sel_blocks(kv[:, :, 3])
    pad = ((0, 0), (WINDOW, 0), (0, 0), (0, 0))
    kw = jnp.pad(kv[:, :, 4], pad)
    vw = jnp.pad(kv[:, :, 5], pad)
    n_blk = T // Q_BLK

    def one_block(args):
        qb, gb, b0 = args
        pos_q = b0 + jnp.arange(Q_BLK, dtype=jnp.int32)
        pos_w = b0 - WINDOW + jnp.arange(WINDOW + Q_BLK, dtype=jnp.int32)
        kwb = lax.dynamic_slice_in_dim(kw, b0, WINDOW + Q_BLK, axis=1)
        vwb = lax.dynamic_slice_in_dim(vw, b0, WINDOW + Q_BLK, axis=1)
        return nsa_attend(qb, gb, pos_q, kc, vc, ks, vs, kwb, vwb, pos_w)

    qs = q.reshape(B, n_blk, Q_BLK, N_HEADS, HEAD_DIM).swapaxes(0, 1)
    gs = gates.reshape(B, n_blk, Q_BLK, N_HEADS, 3).swapaxes(0, 1)
    starts = jnp.arange(n_blk, dtype=jnp.int32) * Q_BLK
    o = lax.map(one_block, (qs, gs, starts)).swapaxes(0, 1).reshape(B, T, Q_WIDTH)
    w_keep = min(WINDOW, T)
    return o @ w_out, kv[:, :, :4], kv[:, T - w_keep:, 4:]


def nsa_sample(h, past_pages, win_buf, w_in, w_out, pe, w1, w2):
    Bd, Ts, _ = h.shape
    pos = PAST_LEN + jnp.arange(Ts, dtype=jnp.int32)
    q, kv, gates = nsa_project(h, pos, w_in)
    past = past_pages.reshape(Bd, -1, 4, N_KV, HEAD_DIM)
    full = jnp.concatenate([past, kv[:, :, :4].astype(past.dtype)], axis=1)
    kc = compress(full[:, :, 0], pe[0], w1[0], w2[0])
    vc = compress(full[:, :, 1], pe[1], w1[1], w2[1])
    ks = sel_blocks(full[:, :, 2])
    vs = sel_blocks(full[:, :, 3])
    w_buf = win_buf.shape[1]
    win_all = jnp.concatenate([win_buf, kv[:, :, 4:].astype(win_buf.dtype)], axis=1)
    pos_w = PAST_LEN - w_buf + jnp.arange(w_buf + Ts, dtype=jnp.int32)
    o = nsa_attend(q, gates, pos, kc, vc, ks, vs, win_all[:, :, 0], win_all[:, :, 1], pos_w)
    return o @ w_out, kv[:, :, :4], win_all[:, Ts:]


def s5_discretize(lam_re, lam_im, log_dt, b_re, b_im):
    f32 = jnp.float32
    lam_re = lam_re.astype(f32)
    lam_im = lam_im.astype(f32)
    dt = jnp.exp(log_dt.astype(f32))[:, None]
    mag = jnp.exp(lam_re * dt)
    ab_re = mag * jnp.cos(lam_im * dt)
    ab_im = mag * jnp.sin(lam_im * dt)
    nr, ni = ab_re - 1.0, ab_im
    den = lam_re * lam_re + lam_im * lam_im
    f_re = ((nr * lam_re + ni * lam_im) / den)[..., None]
    f_im = ((ni * lam_re - nr * lam_im) / den)[..., None]
    b_re = b_re.astype(f32)
    b_im = b_im.astype(f32)
    return ab_re, ab_im, f_re * b_re - f_im * b_im, f_re * b_im + f_im * b_re


def s5_combine(e1, e2):
    a1r, a1i, b1r, b1i = e1
    a2r, a2i, b2r, b2i = e2
    return (a2r * a1r - a2i * a1i, a2r * a1i + a2i * a1r,
            a2r * b1r - a2i * b1i + b2r, a2r * b1i + a2i * b1r + b2i)


def s5_mix(u, h0_re, h0_im, lam_re, lam_im, log_dt, b_re, b_im, c_re, c_im, d_skip, w_glu):
    f32 = jnp.float32
    B, T, _ = u.shape
    ab_re, ab_im, bb_re, bb_im = s5_discretize(lam_re, lam_im, log_dt, b_re, b_im)
    uf = u.astype(f32).reshape(B, T, S5_GROUPS, S5_GROUP)
    bu_re = jnp.einsum('btgc,gnc->btgn', uf, bb_re)
    bu_im = jnp.einsum('btgc,gnc->btgn', uf, bb_im)
    h0_re = h0_re.astype(f32)
    h0_im = h0_im.astype(f32)
    bu_re = bu_re.at[:, 0].add(ab_re * h0_re - ab_im * h0_im)
    bu_im = bu_im.at[:, 0].add(ab_re * h0_im + ab_im * h0_re)
    a_re = jnp.broadcast_to(ab_re, (1, T, S5_GROUPS, S5_STATE))
    a_im = jnp.broadcast_to(ab_im, (1, T, S5_GROUPS, S5_STATE))
    _, _, hr, hi = lax.associative_scan(s5_combine, (a_re, a_im, bu_re, bu_im), axis=1)
    y = (jnp.einsum('btgn,gcn->btgc', hr, c_re.astype(f32))
         - jnp.einsum('btgn,gcn->btgc', hi, c_im.astype(f32))
         + d_skip.astype(f32).reshape(S5_GROUPS, S5_GROUP) * uf)
    z = jax.nn.gelu(y.reshape(B, T, D_MODEL)).astype(u.dtype)
    za, zb = jnp.split(z @ w_glu, 2, axis=-1)
    out = za * jax.nn.sigmoid(zb)
    return out.astype(u.dtype), hr[:, -1], hi[:, -1]


def setup_inputs(seed: int = 0) -> dict:
    key = jax.random.key(seed)
    ks = jax.random.split(key, 26)
    f32 = jnp.float32

    def nrm(k, shape, scale):
        return jax.random.normal(k, shape, f32) * scale

    n_pages = PAST_LEN // PAGE_SIZE
    n_pool = (5 * DEC_BATCH * n_pages) // 4
    w_buf = min(WINDOW, PAST_LEN)
    page_table = jax.random.permutation(ks[6], n_pool)[:DEC_BATCH * n_pages]
    page_table = page_table.reshape(DEC_BATCH, n_pages).astype(jnp.int32)
    return {
        'x_prompt': nrm(ks[0], (BATCH, SEQ, D_MODEL), 1.0),
        'x_sample': nrm(ks[1], (DEC_BATCH, DEC_SEQ, D_MODEL), 1.0),
        'cache_nsa': nrm(ks[2], (N_ATT, n_pool, PAGE_SIZE, 4, N_KV, HEAD_DIM), 1.0),
        'cache_win': nrm(ks[3], (N_ATT, DEC_BATCH, w_buf, 2, N_KV, HEAD_DIM), 1.0),
        'state_s5_re': nrm(ks[4], (N_SSM, DEC_BATCH, S5_GROUPS, S5_STATE), 0.1),
        'state_s5_im': nrm(ks[5], (N_SSM, DEC_BATCH, S5_GROUPS, S5_STATE), 0.1),
        'page_table': page_table,
        'norm_mix': 1.0 + nrm(ks[7], (DEPTH, D_MODEL), 0.01),
        'norm_ffn': 1.0 + nrm(ks[8], (DEPTH, D_MODEL), 0.01),
        'norm_final': 1.0 + nrm(ks[9], (D_MODEL,), 0.01),
        'att_w_in': nrm(ks[10], (N_ATT, D_MODEL, N_IN), D_MODEL ** -0.5),
        'att_w_out': nrm(ks[11], (N_ATT, Q_WIDTH, D_MODEL), Q_WIDTH ** -0.5),
        'cmp_pe': nrm(ks[12], (N_ATT, 2, CMP_LEN, HEAD_DIM), 0.1),
        'cmp_w1': nrm(ks[13], (N_ATT, 2, CMP_LEN * HEAD_DIM, CMP_HID), (CMP_LEN * HEAD_DIM) ** -0.5),
        'cmp_w2': nrm(ks[14], (N_ATT, 2, CMP_HID, HEAD_DIM), CMP_HID ** -0.5),
        's5_lambda_re': -0.5 + nrm(ks[15], (N_SSM, S5_GROUPS, S5_STATE), 0.01),
        's5_lambda_im': jnp.pi * jnp.arange(S5_STATE, dtype=f32) + nrm(ks[16], (N_SSM, S5_GROUPS, S5_STATE), 0.01),
        's5_log_dt': jax.random.uniform(ks[17], (N_SSM, S5_GROUPS), f32, math.log(1e-3), math.log(1e-1)),
        's5_b_re': nrm(ks[18], (N_SSM, S5_GROUPS, S5_STATE, S5_GROUP), (2 * S5_GROUP) ** -0.5),
        's5_b_im': nrm(ks[19], (N_SSM, S5_GROUPS, S5_STATE, S5_GROUP), (2 * S5_GROUP) ** -0.5),
        's5_c_re': nrm(ks[20], (N_SSM, S5_GROUPS, S5_GROUP, S5_STATE), S5_STATE ** -0.5),
        's5_c_im': nrm(ks[21], (N_SSM, S5_GROUPS, S5_GROUP, S5_STATE), S5_STATE ** -0.5),
        's5_d': nrm(ks[22], (N_SSM, D_MODEL), 1.0),
        's5_w_glu': nrm(ks[23], (N_SSM, D_MODEL, 2 * D_MODEL), D_MODEL ** -0.5),
        'ffn_w_gate_up': nrm(ks[24], (DEPTH, D_MODEL, 2 * D_FF), D_MODEL ** -0.5),
        'ffn_w_down': nrm(ks[25], (DEPTH, D_FF, D_MODEL), D_FF ** -0.5),
    }


def reference(x_prompt, x_sample, cache_nsa, cache_win, state_s5_re, state_s5_im, page_table,
              norm_mix, norm_ffn, norm_final, att_w_in, att_w_out, cmp_pe, cmp_w1, cmp_w2,
              s5_lambda_re, s5_lambda_im, s5_log_dt, s5_b_re, s5_b_im, s5_c_re, s5_c_im, s5_d, s5_w_glu,
              ffn_w_gate_up, ffn_w_down):
    hp, hs = x_prompt, x_sample
    kv_p, kv_s, win_p, win_s = [], [], [], []
    sr_p, si_p, sr_s, si_s = [], [], [], []
    for i in range(DEPTH):
        li = i // N_MIXERS
        up = rms_norm(hp, norm_mix[i])
        us = rms_norm(hs, norm_mix[i])
        if i % N_MIXERS == 0:
            w = (att_w_in[li], att_w_out[li], cmp_pe[li], cmp_w1[li], cmp_w2[li])
            yp, rows, win = nsa_prompt(up, *w)
            kv_p.append(rows)
            win_p.append(win)
            ys, rows, win = nsa_sample(us, cache_nsa[li, page_table], cache_win[li], *w)
            kv_s.append(rows)
            win_s.append(win)
        else:
            prm = (s5_lambda_re[li], s5_lambda_im[li], s5_log_dt[li], s5_b_re[li], s5_b_im[li],
                   s5_c_re[li], s5_c_im[li], s5_d[li], s5_w_glu[li])
            h0 = jnp.zeros((hp.shape[0], S5_GROUPS, S5_STATE), jnp.float32)
            yp, hr, hi = s5_mix(up, h0, h0, *prm)
            sr_p.append(hr)
            si_p.append(hi)
            ys, hr, hi = s5_mix(us, state_s5_re[li], state_s5_im[li], *prm)
            sr_s.append(hr)
            si_s.append(hi)
        hp = hp + yp
        hs = hs + ys
        hp = hp + swiglu(rms_norm(hp, norm_ffn[i]), ffn_w_gate_up[i], ffn_w_down[i])
        hs = hs + swiglu(rms_norm(hs, norm_ffn[i]), ffn_w_gate_up[i], ffn_w_down[i])
    return (rms_norm(hp, norm_final), rms_norm(hs, norm_final),
            jnp.stack(kv_p), jnp.stack(kv_s), jnp.stack(win_p), jnp.stack(win_s),
            jnp.stack(sr_p), jnp.stack(si_p), jnp.stack(sr_s), jnp.stack(si_s))
```

```python
import functools

import jax
import jax.numpy as jnp
from jax import lax
from jax.experimental import pallas as pl
from jax.experimental.pallas import tpu as pltpu

EPS = 1e-6
NEG_INF = -1e30
ROPE_THETA = 10000.0
CMP_STRIDE = 16
CMP_LEN = 2 * CMP_STRIDE
SEL_BLOCK = 64
N_SELECT = 16
WINDOW = 512
Q_BLK = 64

LANE = 128
SUBLANE = 8
VMEM_LIMIT_BYTES = 56 * 1024 * 1024

F32 = jnp.float32
BF16 = jnp.bfloat16
_NT = (((1,), (1,)), ((), ()))


def _cparams(semantics):
    return pltpu.CompilerParams(dimension_semantics=semantics, vmem_limit_bytes=VMEM_LIMIT_BYTES)


def _tile(dim, pref, align=LANE):
    if dim <= pref:
        return dim
    t = (pref // align) * align
    while t >= align:
        if dim % t == 0:
            return t
        t -= align
    return dim


def _rmsnorm_kernel(x_ref, g_ref, o_ref):
    x = x_ref[...]
    y = x * lax.rsqrt(jnp.mean(x * x, axis=-1, keepdims=True) + EPS)
    o_ref[...] = (y * g_ref[...]).astype(o_ref.dtype)


def rmsnorm(x, g, out_dtype):
    m, d = x.shape
    tm = _tile(m, 256, SUBLANE)
    return pl.pallas_call(
        _rmsnorm_kernel,
        grid=(m // tm,),
        in_specs=[pl.BlockSpec((tm, d), lambda i: (i, 0)), pl.BlockSpec((1, d), lambda i: (0, 0))],
        out_specs=pl.BlockSpec((tm, d), lambda i: (i, 0)),
        out_shape=jax.ShapeDtypeStruct((m, d), out_dtype),
        compiler_params=_cparams(("parallel",)),
        name="rmsnorm",
    )(x, g.reshape(1, d))


def _mm_kernel(*refs, nk, tn, dual, epilogue, rope_cfg):
    it = iter(refs)
    a_ref = next(it)
    w_ref = next(it)
    w2_ref = next(it) if dual else None
    res_ref = next(it) if epilogue in ("residual", "sglu") else None
    cos_ref = next(it) if epilogue == "rope" else None
    sin_ref = next(it) if epilogue == "rope" else None
    o_ref = next(it)
    acc_ref = next(it)
    acc2_ref = next(it) if dual else None
    k = pl.program_id(2)

    @pl.when(k == 0)
    def _():
        acc_ref[...] = jnp.zeros_like(acc_ref)
        if dual:
            acc2_ref[...] = jnp.zeros_like(acc2_ref)

    a = a_ref[...]
    acc_ref[...] += jnp.dot(a, w_ref[...].astype(BF16), preferred_element_type=F32)
    if dual:
        acc2_ref[...] += jnp.dot(a, w2_ref[...].astype(BF16), preferred_element_type=F32)

    @pl.when(k == nk - 1)
    def _():
        if epilogue == "none":
            o_ref[...] = acc_ref[...].astype(o_ref.dtype)
        elif epilogue == "residual":
            o_ref[...] = (res_ref[...] + acc_ref[...]).astype(o_ref.dtype)
        elif epilogue == "swiglu":
            o_ref[...] = (jax.nn.silu(acc_ref[...]) * acc2_ref[...]).astype(o_ref.dtype)
        elif epilogue == "sglu":
            o_ref[...] = (res_ref[...] + acc_ref[...] * jax.nn.sigmoid(acc2_ref[...])).astype(o_ref.dtype)
        elif epilogue == "rope":
            q_width, kv_width, kvw, hd = rope_cfg
            col0 = pl.program_id(0) * tn
            in_kv = (col0 >= q_width) & (col0 < q_width + kv_width)
            is_rope = (col0 < q_width) | (in_kv & (((col0 - q_width) // kvw) % 2 == 0))

            @pl.when(is_rope)
            def _():
                cos = cos_ref[...]
                sin = sin_ref[...]
                for c in range(tn // hd):
                    x = acc_ref[:, c * hd:(c + 1) * hd]
                    o_ref[:, c * hd:(c + 1) * hd] = x * cos + pltpu.roll(x, hd // 2, 1) * sin

            @pl.when(jnp.logical_not(is_rope))
            def _():
                o_ref[...] = acc_ref[...]


def matmul(a, w, *, tm, tn, tk, epilogue="none", out_dtype=F32, n_out=None, w2_col_off=None,
           res=None, cos=None, sin=None, rope_cfg=None):
    m, kdim = a.shape
    n = w.shape[1] if n_out is None else n_out
    dual = epilogue in ("swiglu", "sglu")
    assert m % tm == 0 and kdim % tk == 0
    nk = kdim // tk
    nj = pl.cdiv(n, tn)
    in_specs = [pl.BlockSpec((tm, tk), lambda j, i, k: (i, k)),
                pl.BlockSpec((tk, tn), lambda j, i, k: (k, j))]
    args = [a, w]
    if dual:
        assert w2_col_off % tn == 0 and n % tn == 0
        off = w2_col_off // tn
        in_specs.append(pl.BlockSpec((tk, tn), lambda j, i, k: (k, j + off)))
        args.append(w)
    if epilogue in ("residual", "sglu"):
        in_specs.append(pl.BlockSpec((tm, tn), lambda j, i, k: (i, j)))
        args.append(res)
    if epilogue == "rope":
        hd = rope_cfg[3]
        in_specs += [pl.BlockSpec((tm, hd), lambda j, i, k: (i, 0))] * 2
        args += [cos, sin]
    scratch = [pltpu.VMEM((tm, tn), F32)]
    if dual:
        scratch.append(pltpu.VMEM((tm, tn), F32))
    return pl.pallas_call(
        functools.partial(_mm_kernel, nk=nk, tn=tn, dual=dual, epilogue=epilogue, rope_cfg=rope_cfg),
        grid=(nj, m // tm, nk),
        in_specs=in_specs,
        out_specs=pl.BlockSpec((tm, tn), lambda j, i, k: (i, j)),
        out_shape=jax.ShapeDtypeStruct((m, n), out_dtype),
        scratch_shapes=scratch,
        compiler_params=_cparams(("parallel", "parallel", "arbitrary")),
        name="mm_" + epilogue,
    )(*args)


def _compress_kernel(x_ref, pe_ref, w1_ref, w2_ref, o_ref, w1_s, w2_s):
    @pl.when(pl.program_id(0) == 0)
    def _():
        w1_s[...] = w1_ref[...].astype(BF16)
        w2_s[...] = w2_ref[...].astype(BF16)

    x = (x_ref[...] + pe_ref[...]).astype(BF16)
    h = jax.nn.gelu(jnp.dot(x, w1_s[...], preferred_element_type=F32))
    o_ref[...] = jnp.dot(h.astype(BF16), w2_s[...], preferred_element_type=F32).astype(o_ref.dtype)


def compress_rows(flat, pe, w1, w2):
    r, kdim = flat.shape
    hid, hd = w2.shape
    tr = _tile(r, 512, SUBLANE)
    return pl.pallas_call(
        _compress_kernel,
        grid=(r // tr,),
        in_specs=[pl.BlockSpec((tr, kdim), lambda i: (i, 0)),
                  pl.BlockSpec((1, kdim), lambda i: (0, 0)),
                  pl.BlockSpec((kdim, hid), lambda i: (0, 0)),
                  pl.BlockSpec((hid, hd), lambda i: (0, 0))],
        out_specs=pl.BlockSpec((tr, hd), lambda i: (i, 0)),
        out_shape=jax.ShapeDtypeStruct((r, hd), BF16),
        scratch_shapes=[pltpu.VMEM((kdim, hid), BF16), pltpu.VMEM((hid, hd), BF16)],
        compiler_params=_cparams(("arbitrary",)),
        name="compress",
    )(flat, pe.reshape(1, kdim), w1, w2)


def _cmp_flat(x, ncp):
    b, tk, n_kv, hd = x.shape
    n_chunk = tk // CMP_STRIDE
    c = x[:, :n_chunk * CMP_STRIDE].reshape(b, n_chunk, CMP_STRIDE, n_kv, hd)
    blk = jnp.concatenate([c[:, :-1], c[:, 1:]], axis=2)
    flat = blk.transpose(0, 3, 1, 2, 4).reshape(b, n_kv, n_chunk - 1, CMP_LEN * hd)
    return jnp.pad(flat, ((0, 0), (0, 0), (0, ncp - (n_chunk - 1)), (0, 0)))


def _overlap_matrix(ncp, nsp):
    ic = jnp.arange(ncp, dtype=jnp.int32)[:, None] * CMP_STRIDE
    js = jnp.arange(nsp, dtype=jnp.int32)[None, :] * SEL_BLOCK
    return ((ic < js + SEL_BLOCK) & (ic + CMP_LEN > js)).astype(F32)


def _softmax_probs(s, mask):
    s = jnp.where(mask, s, NEG_INF)
    e = jnp.where(mask, jnp.exp(s - jnp.max(s, axis=-1, keepdims=True)), 0.0)
    l = jnp.sum(e, axis=-1, keepdims=True)
    return e * jnp.where(l > 0.0, 1.0 / l, 0.0)


def _nsa_prompt_kernel(q_ref, ks_ref, vs_ref, kw_ref, vw_ref, kc_ref, vc_ref, gate_ref, ovl_ref, exp_ref,
                       o_ref, ks_s, vs_s, kw_s, vw_s, *, gqa, hd, t_len, n_cmp, n_sel, k_sel, scale):
    qb = pl.program_id(2)

    @pl.when(qb == 0)
    def _():
        ks_s[...] = ks_ref[...].astype(BF16)
        vs_s[...] = vs_ref[...].astype(BF16)
        kw_s[...] = kw_ref[...].astype(BF16)
        vw_s[...] = vw_ref[...].astype(BF16)

    rows = gqa * Q_BLK
    b0 = qb * Q_BLK
    q = q_ref[...]
    qg = jnp.concatenate([q[:, g * hd:(g + 1) * hd] for g in range(gqa)], axis=0).astype(BF16)
    pos_q = b0 + lax.broadcasted_iota(jnp.int32, (Q_BLK, 1), 0)

    ncp = kc_ref.shape[0]
    s = lax.dot_general(qg, kc_ref[...], _NT, preferred_element_type=F32) * scale
    ci = lax.broadcasted_iota(jnp.int32, (1, ncp), 1)
    m_cmp = ((ci * CMP_STRIDE + (CMP_LEN - 1)) <= pos_q) & (ci < n_cmp)
    s3 = s.reshape(gqa, Q_BLK, ncp) + jnp.where(m_cmp, 0.0, NEG_INF)[None]
    e3 = jnp.exp(s3 - jnp.max(s3, axis=-1, keepdims=True)) * m_cmp.astype(F32)[None]
    l3 = jnp.sum(e3, axis=-1, keepdims=True)
    p3 = e3 * jnp.where(l3 > 0.0, 1.0 / l3, 0.0)
    o_cmp = jnp.dot(p3.reshape(rows, ncp).astype(BF16), vc_ref[...], preferred_element_type=F32)

    nsp = ovl_ref.shape[1]
    imp = jnp.dot(jnp.sum(p3, axis=0), ovl_ref[...], preferred_element_type=F32,
                  precision=lax.Precision.HIGHEST)
    jj = lax.broadcasted_iota(jnp.int32, (1, nsp), 1)
    cur = pos_q // SEL_BLOCK
    forced = (jj == 0) | (jj == cur) | (jj == cur - 1)
    future = jj * SEL_BLOCK > pos_q
    score = jnp.where(future, -jnp.inf, jnp.where(forced, jnp.inf, imp))
    rank = jnp.zeros((Q_BLK, nsp), F32)
    for j2 in range(n_sel):
        col = score[:, j2:j2 + 1]
        beats = (col > score) | ((col == score) & (j2 < jj))
        rank = rank + beats.astype(F32)
    sel = ((rank < k_sel) & (jj < n_sel)).astype(BF16)
    selk = jnp.dot(sel, exp_ref[...], preferred_element_type=F32)

    s = lax.dot_general(qg, ks_s[...], _NT, preferred_element_type=F32) * scale
    kpos = lax.broadcasted_iota(jnp.int32, (1, t_len), 1)
    m_sel = (selk > 0.5) & (kpos <= pos_q)
    s3 = s.reshape(gqa, Q_BLK, t_len) + jnp.where(m_sel, 0.0, NEG_INF)[None]
    e3 = jnp.exp(s3 - jnp.max(s3, axis=-1, keepdims=True))
    l_sel = jnp.sum(e3, axis=-1, keepdims=True).reshape(rows, 1)
    o_sel = jnp.dot(e3.reshape(rows, t_len).astype(BF16), vs_s[...], preferred_element_type=F32) / l_sel

    wk = WINDOW + Q_BLK
    start = pl.multiple_of(jnp.maximum(b0 - WINDOW, 0), Q_BLK)
    s = lax.dot_general(qg, kw_s[pl.ds(start, wk), :], _NT, preferred_element_type=F32) * scale
    dist = pos_q - (start + lax.broadcasted_iota(jnp.int32, (1, wk), 1))
    m_win = (dist >= 0) & (dist <= WINDOW)
    s3 = s.reshape(gqa, Q_BLK, wk) + jnp.where(m_win, 0.0, NEG_INF)[None]
    e3 = jnp.exp(s3 - jnp.max(s3, axis=-1, keepdims=True))
    l_win = jnp.sum(e3, axis=-1, keepdims=True).reshape(rows, 1)
    o_win = jnp.dot(e3.reshape(rows, wk).astype(BF16), vw_s[pl.ds(start, wk), :],
                    preferred_element_type=F32) / l_win

    gate = jax.nn.sigmoid(gate_ref[...])
    for g in range(gqa):
        r = slice(g * Q_BLK, (g + 1) * Q_BLK)
        o = (gate[:, 3 * g:3 * g + 1] * o_cmp[r] + gate[:, 3 * g + 1:3 * g + 2] * o_sel[r]
             + gate[:, 3 * g + 2:3 * g + 3] * o_win[r])
        o_ref[:, g * hd:(g + 1) * hd] = o.astype(o_ref.dtype)


def nsa_prompt_attention(proj, gates_r, kc, vc, *, batch, t_len, n_kv, gqa, hd, n_cmp):
    m = batch * t_len
    nq = t_len // Q_BLK
    q_width = n_kv * gqa * hd
    kvw = n_kv * hd
    ncp = kc.shape[2]
    n_sel = -(-t_len // SEL_BLOCK)
    nsp = -(-n_sel // LANE) * LANE
    k_sel = min(N_SELECT, n_sel)
    assert t_len % Q_BLK == 0 and t_len >= WINDOW + Q_BLK and t_len % SEL_BLOCK == 0
    ovl = _overlap_matrix(ncp, nsp)
    expand = (jnp.arange(nsp, dtype=jnp.int32)[:, None]
              == jnp.arange(t_len, dtype=jnp.int32)[None, :] // SEL_BLOCK).astype(BF16)

    def kv_spec(slot):
        base = (q_width + slot * kvw) // hd
        return pl.BlockSpec((t_len, hd), lambda b, k, i: (b, base + k))

    kern = functools.partial(_nsa_prompt_kernel, gqa=gqa, hd=hd, t_len=t_len, n_cmp=n_cmp, n_sel=n_sel,
                             k_sel=k_sel, scale=hd ** -0.5)
    return pl.pallas_call(
        kern,
        grid=(batch, n_kv, nq),
        in_specs=[pl.BlockSpec((Q_BLK, gqa * hd), lambda b, k, i: (b * nq + i, k)),
                  kv_spec(2), kv_spec(3), kv_spec(4), kv_spec(5),
                  pl.BlockSpec((None, None, ncp, hd), lambda b, k, i: (b, k, 0, 0)),
                  pl.BlockSpec((None, None, ncp, hd), lambda b, k, i: (b, k, 0, 0)),
                  pl.BlockSpec((None, Q_BLK, 3 * gqa), lambda b, k, i: (k, b * nq + i, 0)),
                  pl.BlockSpec((ncp, nsp), lambda b, k, i: (0, 0)),
                  pl.BlockSpec((nsp, t_len), lambda b, k, i: (0, 0))],
        out_specs=pl.BlockSpec((Q_BLK, gqa * hd), lambda b, k, i: (b * nq + i, k)),
        out_shape=jax.ShapeDtypeStruct((m, q_width), BF16),
        scratch_shapes=[pltpu.VMEM((t_len, hd), BF16)] * 4,
        compiler_params=_cparams(("parallel", "parallel", "arbitrary")),
        name="nsa_prompt",
    )(proj, proj, proj, proj, proj, kc, vc, gates_r, ovl, expand)


def _nsa_dec_select_kernel(q_ref, kc_ref, vc_ref, ovl_ref, ocmp_ref, sel_ref, *, n_cmp, n_sel, k_sel, pos_q,
                           scale):
    q = q_ref[...].astype(BF16)
    ncp = kc_ref.shape[0]
    nsp = ovl_ref.shape[1]
    s = lax.dot_general(q, kc_ref[...], _NT, preferred_element_type=F32) * scale
    ci = lax.broadcasted_iota(jnp.int32, (1, ncp), 1)
    m_cmp = ((ci * CMP_STRIDE + (CMP_LEN - 1)) <= pos_q) & (ci < n_cmp)
    p = _softmax_probs(s, m_cmp)
    ocmp_ref[...] = jnp.dot(p.astype(BF16), vc_ref[...], preferred_element_type=F32)

    imp = jnp.sum(jnp.dot(p, ovl_ref[...], preferred_element_type=F32, precision=lax.Precision.HIGHEST),
                  axis=0, keepdims=True)
    jj = lax.broadcasted_iota(jnp.int32, (1, nsp), 1)
    cur = pos_q // SEL_BLOCK
    forced = (jj == 0) | (jj == cur) | (jj == cur - 1)
    future = jj * SEL_BLOCK > pos_q
    score = jnp.where(future, -jnp.inf, jnp.where(forced, jnp.inf, imp))
    rr = lax.broadcasted_iota(jnp.int32, (nsp, nsp), 0)
    cc = lax.broadcasted_iota(jnp.int32, (nsp, nsp), 1)
    score_b = jnp.broadcast_to(score, (nsp, nsp))
    score_col = jnp.sum(jnp.where(rr == cc, score_b, 0.0), axis=1, keepdims=True)
    beats = ((score_col > score_b) | ((score_col == score_b) & (rr < cc))) & (rr < n_sel)
    rank = jnp.sum(beats.astype(F32), axis=0, keepdims=True)
    sel_ref[...] = ((rank < k_sel) & (jj < n_sel)).astype(F32)


def nsa_dec_select(q, kc, vc, *, n_cmp, n_sel, pos_q):
    bd, n_kv, gqa, hd = q.shape
    ncp = kc.shape[2]
    nsp = -(-n_sel // LANE) * LANE
    k_sel = min(N_SELECT, n_sel)
    kern = functools.partial(_nsa_dec_select_kernel, n_cmp=n_cmp, n_sel=n_sel, k_sel=k_sel, pos_q=pos_q,
                             scale=hd ** -0.5)
    return pl.pallas_call(
        kern,
        grid=(bd, n_kv),
        in_specs=[pl.BlockSpec((None, None, gqa, hd), lambda b, k: (b, k, 0, 0)),
                  pl.BlockSpec((None, None, ncp, hd), lambda b, k: (b, k, 0, 0)),
                  pl.BlockSpec((None, None, ncp, hd), lambda b, k: (b, k, 0, 0)),
                  pl.BlockSpec((ncp, nsp), lambda b, k: (0, 0))],
        out_specs=[pl.BlockSpec((None, None, gqa, hd), lambda b, k: (b, k, 0, 0)),
                   pl.BlockSpec((None, None, 1, nsp), lambda b, k: (b, k, 0, 0))],
        out_shape=[jax.ShapeDtypeStruct((bd, n_kv, gqa, hd), F32),
                   jax.ShapeDtypeStruct((bd, n_kv, 1, nsp), F32)],
        compiler_params=_cparams(("parallel", "parallel")),
        name="nsa_dec_select",
    )(q, kc, vc, _overlap_matrix(ncp, nsp))


def _nsa_dec_attend_kernel(pt_ref, q_ref, kpg_ref, vpg_ref, selk_ref, new_ref, kw_ref, vw_ref, ocmp_ref,
                           gate_ref, o_ref, m_s, l_s, acc_s, *, n_pages, page, past_len, pos_q, scale):
    del pt_ref
    p = pl.program_id(2)

    @pl.when(p == 0)
    def _():
        m_s[...] = jnp.full_like(m_s, NEG_INF)
        l_s[...] = jnp.zeros_like(l_s)
        acc_s[...] = jnp.zeros_like(acc_s)

    q = q_ref[...].astype(BF16)

    @pl.when(p < n_pages)
    def _():
        s = lax.dot_general(q, kpg_ref[...].astype(BF16), _NT, preferred_element_type=F32) * scale
        kpos = p * page + lax.broadcasted_iota(jnp.int32, (1, page), 1)
        msk = (selk_ref[...] > 0.5) & (kpos <= pos_q)
        s = jnp.where(msk, s, NEG_INF)
        m_new = jnp.maximum(m_s[...], jnp.max(s, axis=-1, keepdims=True))
        alpha = jnp.exp(m_s[...] - m_new)
        e = jnp.where(msk, jnp.exp(s - m_new), 0.0)
        l_s[...] = alpha * l_s[...] + jnp.sum(e, axis=-1, keepdims=True)
        acc_s[...] = alpha * acc_s[...] + jnp.dot(e.astype(BF16), vpg_ref[...].astype(BF16),
                                                  preferred_element_type=F32)
        m_s[...] = m_new

    @pl.when(p == n_pages)
    def _():
        new = new_ref[...].astype(BF16)
        s_all = lax.dot_general(q, new, _NT, preferred_element_type=F32) * scale
        newf = new.astype(F32)

        msk = (selk_ref[:, 0:1] > 0.5) & (past_len <= pos_q)
        s_new = jnp.where(msk, s_all[:, 2:3], NEG_INF)
        m_new = jnp.maximum(m_s[...], s_new)
        alpha = jnp.exp(m_s[...] - m_new)
        e = jnp.where(msk, jnp.exp(s_new - m_new), 0.0)
        l_fin = alpha * l_s[...] + e
        acc = alpha * acc_s[...] + e.astype(BF16).astype(F32) * newf[3:4, :]
        o_sel = acc / l_fin

        wb = kw_ref.shape[0]
        s = lax.dot_general(q, kw_ref[...].astype(BF16), _NT, preferred_element_type=F32) * scale
        pos_w = (past_len - wb) + lax.broadcasted_iota(jnp.int32, (1, wb), 1)
        dist = pos_q - pos_w
        m_win = (dist >= 0) & (dist <= WINDOW) & (pos_w >= 0)
        s = jnp.where(m_win, s, NEG_INF)
        s_nw = s_all[:, 4:5]
        mx = jnp.maximum(jnp.max(s, axis=-1, keepdims=True), s_nw)
        e = jnp.where(m_win, jnp.exp(s - mx), 0.0)
        e_nw = jnp.exp(s_nw - mx)
        l_w = jnp.sum(e, axis=-1, keepdims=True) + e_nw
        o_win = (jnp.dot(e.astype(BF16), vw_ref[...].astype(BF16), preferred_element_type=F32)
                 + e_nw.astype(BF16).astype(F32) * newf[5:6, :]) / l_w

        gate = jax.nn.sigmoid(gate_ref[...])
        o = gate[:, 0:1] * ocmp_ref[...] + gate[:, 1:2] * o_sel + gate[:, 2:3] * o_win
        o_ref[...] = o.astype(o_ref.dtype)


def nsa_dec_attend(page_table, q, cache2d, selk, new_rows, win2d, o_cmp, gates, *, past_len, pos_q):
    bd, n_kv, gqa, hd = q.shape
    n_pages = page_table.shape[1]
    page = cache2d.shape[1]
    wb = win2d.shape[1]
    kvw = n_kv * hd
    kern = functools.partial(_nsa_dec_attend_kernel, n_pages=n_pages, page=page, past_len=past_len, pos_q=pos_q,
                             scale=hd ** -0.5)

    def page_spec(slot):
        base = slot * kvw // hd
        return pl.BlockSpec(
            (None, page, hd),
            lambda b, k, p, pt: (pt[b * n_pages + jnp.minimum(p, n_pages - 1)], 0, base + k))

    grid_spec = pltpu.PrefetchScalarGridSpec(
        num_scalar_prefetch=1,
        grid=(bd, n_kv, n_pages + 1),
        in_specs=[pl.BlockSpec((None, None, gqa, hd), lambda b, k, p, pt: (b, k, 0, 0)),
                  page_spec(2), page_spec(3),
                  pl.BlockSpec((None, None, None, 1, page), lambda b, k, p, pt: (b, k, p, 0, 0)),
                  pl.BlockSpec((None, None, SUBLANE, hd), lambda b, k, p, pt: (b, k, 0, 0)),
                  pl.BlockSpec((None, wb, hd), lambda b, k, p, pt: (b, 0, k)),
                  pl.BlockSpec((None, wb, hd), lambda b, k, p, pt: (b, 0, n_kv + k)),
                  pl.BlockSpec((None, None, gqa, hd), lambda b, k, p, pt: (b, k, 0, 0)),
                  pl.BlockSpec((None, None, gqa, 3), lambda b, k, p, pt: (b, k, 0, 0))],
        out_specs=pl.BlockSpec((None, None, gqa, hd), lambda b, k, p, pt: (b, k, 0, 0)),
        scratch_shapes=[pltpu.VMEM((gqa, 1), F32), pltpu.VMEM((gqa, 1), F32), pltpu.VMEM((gqa, hd), F32)],
    )
    return pl.pallas_call(
        kern,
        grid_spec=grid_spec,
        out_shape=jax.ShapeDtypeStruct((bd, n_kv, gqa, hd), BF16),
        compiler_params=_cparams(("parallel", "parallel", "arbitrary")),
        name="nsa_dec_attend",
    )(page_table.reshape(-1), q, cache2d, cache2d, selk, new_rows, win2d, win2d, o_cmp, gates)


N_POW = SUBLANE


def _s5_disc_kernel(lre_ref, lim_ref, ldt_ref, bre_ref, bim_ref, bbre_ref, bbim_ref, pre_ref, pim_ref):
    lam_re = lre_ref[...]
    lam_im = lim_ref[...]
    dt = jnp.exp(ldt_ref[...])
    mag = jnp.exp(lam_re * dt)
    ab_re = mag * jnp.cos(lam_im * dt)
    ab_im = mag * jnp.sin(lam_im * dt)
    nr, ni = ab_re - 1.0, ab_im
    den = lam_re * lam_re + lam_im * lam_im
    f_re = (nr * lam_re + ni * lam_im) / den
    f_im = (ni * lam_re - nr * lam_im) / den
    b_re = bre_ref[...]
    b_im = bim_ref[...]
    bbre_ref[...] = f_re * b_re - f_im * b_im
    bbim_ref[...] = f_re * b_im + f_im * b_re
    pr, pi = ab_re, ab_im
    pre_ref[0] = pr
    pim_ref[0] = pi
    for n in range(1, N_POW):
        pr, pi = pr * ab_re - pi * ab_im, pr * ab_im + pi * ab_re
        pre_ref[n] = pr
        pim_ref[n] = pi


def s5_discretize(lam_re, lam_im, log_dt, b_re, b_im):
    g, n, c = b_re.shape
    rep = lambda x: jnp.broadcast_to(x[:, None, :], (g, c, n)).reshape(g * c, n)
    ldt = jnp.broadcast_to(log_dt[:, None, None], (g, c, n)).reshape(g * c, n)
    bt = lambda x: x.transpose(0, 2, 1).reshape(g * c, n)
    full = pl.BlockSpec((g * c, n), lambda: (0, 0))
    pw = pl.BlockSpec((N_POW, g * c, n), lambda: (0, 0, 0))
    bbre, bbim, pre, pim = pl.pallas_call(
        _s5_disc_kernel,
        in_specs=[full] * 5,
        out_specs=[full, full, pw, pw],
        out_shape=[jax.ShapeDtypeStruct((g * c, n), F32)] * 2
        + [jax.ShapeDtypeStruct((N_POW, g * c, n), F32)] * 2,
        compiler_params=pltpu.CompilerParams(vmem_limit_bytes=VMEM_LIMIT_BYTES),
        name="s5_discretize",
    )(rep(lam_re), rep(lam_im), ldt, bt(b_re), bt(b_im))
    pick = lambda x: x.reshape(N_POW, g, c, n)[:, :, 0, :]
    return bbre.reshape(g, c, n), bbim.reshape(g, c, n), pick(pre), pick(pim)


def _s5_scan_kernel(u_ref, h0r_ref, h0i_ref, bbr_ref, bbi_ref, ccr_ref, cci_ref, d_ref, mr_ref, mi_ref,
                    pr_ref, pi_ref, z_ref, hr_ref, hi_ref, xr_s, xi_s, cr_s, ci_s, *, tt, n_t, last_row):
    t = pl.program_id(2)

    @pl.when(t == 0)
    def _():
        cr_s[...] = h0r_ref[...]
        ci_s[...] = h0i_ref[...]

    u = u_ref[...]
    ub = u.astype(BF16)
    xr_s[...] = jnp.dot(ub, bbr_ref[...].astype(BF16), preferred_element_type=F32)
    xi_s[...] = jnp.dot(ub, bbi_ref[...].astype(BF16), preferred_element_type=F32)
    pr = pr_ref[...]
    pi = pi_ref[...]

    def group(r, carry):
        cr, ci = carry
        rows = pl.ds(pl.multiple_of(r * SUBLANE, SUBLANE), SUBLANE)
        xr = xr_s[rows, :]
        xi = xi_s[rows, :]
        for n, sh in enumerate((1, 2, 4)):
            mr = mr_ref[n]
            mi = mi_ref[n]
            sr = pltpu.roll(xr, sh, 0)
            si = pltpu.roll(xi, sh, 0)
            xr, xi = xr + (mr * sr - mi * si), xi + (mr * si + mi * sr)
        xr, xi = xr + (pr * cr - pi * ci), xi + (pr * ci + pi * cr)
        xr_s[rows, :] = xr
        xi_s[rows, :] = xi
        return xr[SUBLANE - 1:SUBLANE, :], xi[SUBLANE - 1:SUBLANE, :]

    cr, ci = lax.fori_loop(0, tt // SUBLANE, group, (cr_s[...], ci_s[...]))
    cr_s[...] = cr
    ci_s[...] = ci

    y = (jnp.dot(xr_s[...].astype(BF16), ccr_ref[...].astype(BF16), preferred_element_type=F32)
         - jnp.dot(xi_s[...].astype(BF16), cci_ref[...].astype(BF16), preferred_element_type=F32)
         + d_ref[...] * u)
    z_ref[...] = jax.nn.gelu(y).astype(z_ref.dtype)

    @pl.when(t == n_t - 1)
    def _():
        hr_ref[...] = xr_s[last_row:last_row + 1, :]
        hi_ref[...] = xi_s[last_row:last_row + 1, :]


def s5_scan(u, h0_re, h0_im, prm, *, batch, t_len, t_valid):
    bb_re, bb_im, cc_re, cc_im, d_skip, m_re, m_im, p_re, p_im = prm
    n_slab, cw, sw = bb_re.shape
    tt = _tile(t_len, 256, SUBLANE)
    n_t = t_len // tt
    assert t_valid > (n_t - 1) * tt
    last_row = (t_valid - 1) % tt
    h0r = h0_re.reshape(batch, n_slab, 1, sw)
    h0i = h0_im.reshape(batch, n_slab, 1, sw)
    slab3 = lambda r, c: pl.BlockSpec((None, r, c), lambda b, s, t: (s, 0, 0))
    st_spec = pl.BlockSpec((None, None, 1, sw), lambda b, s, t: (b, s, 0, 0))
    tab4 = lambda r: pl.BlockSpec((None, r, SUBLANE, sw), lambda b, s, t: (s, 0, 0, 0))
    kern = functools.partial(_s5_scan_kernel, tt=tt, n_t=n_t, last_row=last_row)
    z, hr, hi = pl.pallas_call(
        kern,
        grid=(batch, n_slab, n_t),
        in_specs=[pl.BlockSpec((tt, cw), lambda b, s, t: (b * n_t + t, s)),
                  st_spec, st_spec,
                  slab3(cw, sw), slab3(cw, sw), slab3(sw, cw), slab3(sw, cw),
                  pl.BlockSpec((1, cw), lambda b, s, t: (0, s)),
                  tab4(3), tab4(3),
                  slab3(SUBLANE, sw), slab3(SUBLANE, sw)],
        out_specs=[pl.BlockSpec((tt, cw), lambda b, s, t: (b * n_t + t, s)), st_spec, st_spec],
        out_shape=[jax.ShapeDtypeStruct((batch * t_len, n_slab * cw), BF16),
                   jax.ShapeDtypeStruct((batch, n_slab, 1, sw), F32),
                   jax.ShapeDtypeStruct((batch, n_slab, 1, sw), F32)],
        scratch_shapes=[pltpu.VMEM((tt, sw), F32), pltpu.VMEM((tt, sw), F32),
                        pltpu.VMEM((1, sw), F32), pltpu.VMEM((1, sw), F32)],
        compiler_params=_cparams(("parallel", "parallel", "arbitrary")),
        name="s5_scan",
    )(u, h0r, h0i, bb_re, bb_im, cc_re, cc_im, d_skip.reshape(1, -1), m_re, m_im, p_re, p_im)
    return z, hr, hi


def s5_prepare(lam_re, lam_im, log_dt, b_re, b_im, c_re, c_im, d_skip):
    g, n, c = b_re.shape
    gps = LANE // c
    n_slab = g // gps
    sw = gps * n
    bbt_re, bbt_im, pw_re, pw_im = s5_discretize(lam_re, lam_im, log_dt, b_re, b_im)
    eye = jnp.eye(gps, dtype=F32)

    def blockdiag_in(bt):
        x = bt.reshape(n_slab, gps, c, 1, n) * eye[None, :, None, :, None]
        return x.reshape(n_slab, gps * c, sw)

    def blockdiag_out(cm):
        x = cm.transpose(0, 2, 1).reshape(n_slab, gps, n, 1, c) * eye[None, :, None, :, None]
        return x.reshape(n_slab, sw, gps * c)

    pw_re = pw_re.reshape(N_POW, n_slab, sw)
    pw_im = pw_im.reshape(N_POW, n_slab, sw)
    row = jnp.arange(SUBLANE)[None, :, None]

    def shift_tables(pw):
        tabs = [jnp.where(row >= sh, pw[sh - 1][:, None, :], 0.0) for sh in (1, 2, 4)]
        return jnp.stack(tabs, axis=1)

    carry = lambda pw: pw.transpose(1, 0, 2)
    return (blockdiag_in(bbt_re), blockdiag_in(bbt_im), blockdiag_out(c_re.astype(F32)),
            blockdiag_out(c_im.astype(F32)), d_skip.astype(F32), shift_tables(pw_re), shift_tables(pw_im),
            carry(pw_re), carry(pw_im))


def _rope_tables(pos, hd):
    half = hd // 2
    inv = ROPE_THETA ** (-jnp.arange(half, dtype=F32) / half)
    ang = pos.astype(F32)[:, None] * inv[None, :]
    cos = jnp.cos(ang)
    sin = jnp.sin(ang)
    return jnp.concatenate([cos, cos], axis=-1), jnp.concatenate([-sin, sin], axis=-1)


def _mm_tiles(m, kdim, n):
    tm = _tile(m, 1024, SUBLANE)
    return tm, _tile(n, 512), _tile(kdim, 1024)


def in_proj(u, w_in, pos, *, q_width, kv_width, kvw, hd):
    m, kdim = u.shape
    cos, sin = _rope_tables(pos, hd)
    tm = _tile(m, 1024, SUBLANE)
    return matmul(u, w_in, tm=tm, tn=kvw, tk=_tile(kdim, 1024), epilogue="rope", cos=cos, sin=sin,
                  rope_cfg=(q_width, kv_width, kvw, hd))


def ffn(h, hn, w_gu, w_down):
    m, d = h.shape
    d_ff = w_down.shape[0]
    tm = _tile(m, 1024, SUBLANE)
    act = matmul(hn, w_gu, tm=tm, tn=_tile(d_ff, 512), tk=_tile(d, 1024), epilogue="swiglu", out_dtype=BF16,
                 n_out=d_ff, w2_col_off=d_ff)
    return matmul(act, w_down, tm=tm, tn=_tile(d, 1024), tk=_tile(d_ff, 512), epilogue="residual", res=h)


def kernel(x_prompt, x_sample, cache_nsa, cache_win, state_s5_re, state_s5_im, page_table, norm_mix, norm_ffn,
           norm_final, att_w_in, att_w_out, cmp_pe, cmp_w1, cmp_w2, s5_lambda_re, s5_lambda_im, s5_log_dt,
           s5_b_re, s5_b_im, s5_c_re, s5_c_im, s5_d, s5_w_glu, ffn_w_gate_up, ffn_w_down):
    batch, t_len, d = x_prompt.shape
    bd, ts, _ = x_sample.shape
    _, n_pool, page, _, n_kv, hd = cache_nsa.shape
    wb = cache_win.shape[2]
    n_pages = page_table.shape[1]
    past_len = n_pages * page
    depth = norm_mix.shape[0]
    n_heads = d // hd
    gqa = n_heads // n_kv
    q_width = n_heads * hd
    kvw = n_kv * hd
    kv_width = 6 * kvw
    assert ts == 1 and hd == LANE and wb == min(WINDOW, past_len)
    mp = batch * t_len

    hp = x_prompt.reshape(mp, d)
    hs = x_sample.reshape(bd, d)
    kv_p, kv_s, win_p, win_s = [], [], [], []
    sr_p, si_p, sr_s, si_s = [], [], [], []
    pos_p = jnp.tile(jnp.arange(t_len, dtype=jnp.int32), batch)
    pos_s = jnp.full((bd,), past_len, jnp.int32)

    for i in range(depth):
        li = i // 2
        if i % 2 == 0:
            up = rmsnorm(hp, norm_mix[i], BF16)
            us = rmsnorm(hs, norm_mix[i], BF16)
            w_in, w_out, pe, w1, w2 = att_w_in[li], att_w_out[li], cmp_pe[li], cmp_w1[li], cmp_w2[li]
            cfg = dict(q_width=q_width, kv_width=kv_width, kvw=kvw, hd=hd)

            proj = in_proj(up, w_in, pos_p, **cfg)
            proj3 = proj.reshape(batch, t_len, -1)
            kv_p.append(proj3[:, :, q_width:q_width + 4 * kvw].reshape(batch, t_len, 4, n_kv, hd))
            w_keep = min(WINDOW, t_len)
            win_p.append(proj3[:, t_len - w_keep:, q_width + 4 * kvw:q_width + 6 * kvw]
                         .reshape(batch, w_keep, 2, n_kv, hd))
            n_cmp = t_len // CMP_STRIDE - 1
            ncp = -(-n_cmp // LANE) * LANE
            kc, vc = [
                compress_rows(
                    _cmp_flat(proj3[:, :, q_width + s * kvw:q_width + (s + 1) * kvw]
                              .reshape(batch, t_len, n_kv, hd), ncp).reshape(batch * n_kv * ncp, CMP_LEN * hd),
                    pe[s], w1[s], w2[s]).reshape(batch, n_kv, ncp, hd)
                for s in (0, 1)]
            gates_r = proj[:, q_width + kv_width:].reshape(mp, n_kv, 3 * gqa).transpose(1, 0, 2)
            o = nsa_prompt_attention(proj, gates_r, kc, vc, batch=batch, t_len=t_len, n_kv=n_kv, gqa=gqa,
                                     hd=hd, n_cmp=n_cmp)
            tm, tn, tk = _mm_tiles(mp, q_width, d)
            hp = matmul(o, w_out, tm=tm, tn=tn, tk=tk, epilogue="residual", res=hp)

            proj_s = in_proj(us, w_in, pos_s, **cfg)
            rows_s = proj_s[:, q_width:q_width + kv_width].reshape(bd, 6, n_kv, hd)
            kv_s.append(rows_s[:, None, :4])
            win_s.append(jnp.concatenate([cache_win[li][:, ts:], rows_s[:, None, 4:]], axis=1))
            tk_len = past_len + ts
            n_cmp_s = tk_len // CMP_STRIDE - 1
            ncp_s = -(-n_cmp_s // LANE) * LANE
            past = cache_nsa[li][page_table].reshape(bd, past_len, 4, n_kv, hd)
            kcs, vcs = [
                compress_rows(
                    _cmp_flat(jnp.concatenate([past[:, :, s], rows_s[:, None, s]], axis=1), ncp_s)
                    .reshape(bd * n_kv * ncp_s, CMP_LEN * hd),
                    pe[s], w1[s], w2[s]).reshape(bd, n_kv, ncp_s, hd)
                for s in (0, 1)]
            q_s = proj_s[:, :q_width].reshape(bd, n_kv, gqa, hd)
            n_sel_s = -(-tk_len // SEL_BLOCK)
            o_cmp, sel = nsa_dec_select(q_s, kcs, vcs, n_cmp=n_cmp_s, n_sel=n_sel_s, pos_q=past_len)
            selk = jnp.repeat(sel[:, :, 0, :n_sel_s], SEL_BLOCK, axis=-1)
            selk = jnp.pad(selk, ((0, 0), (0, 0), (0, (n_pages + 1) * page - n_sel_s * SEL_BLOCK)))
            selk = selk.reshape(bd, n_kv, n_pages + 1, 1, page)
            new_rows = jnp.pad(rows_s.transpose(0, 2, 1, 3), ((0, 0), (0, 0), (0, SUBLANE - 6), (0, 0)))
            gates_s = proj_s[:, q_width + kv_width:].reshape(bd, n_kv, gqa, 3)
            o_s = nsa_dec_attend(page_table, q_s, cache_nsa[li].reshape(n_pool, page, 4 * kvw), selk, new_rows,
                                 cache_win[li].reshape(bd, wb, 2 * kvw), o_cmp, gates_s,
                                 past_len=past_len, pos_q=past_len)
            tm, tn, tk = _mm_tiles(bd, q_width, d)
            hs = matmul(o_s.reshape(bd, q_width), w_out, tm=tm, tn=tn, tk=tk, epilogue="residual", res=hs)
        else:
            up = rmsnorm(hp, norm_mix[i], F32)
            us = rmsnorm(hs, norm_mix[i], F32)
            prm = s5_prepare(s5_lambda_re[li], s5_lambda_im[li], s5_log_dt[li], s5_b_re[li], s5_b_im[li],
                             s5_c_re[li], s5_c_im[li], s5_d[li])
            g_cnt, n_state = s5_lambda_re.shape[1:]
            w_glu = s5_w_glu[li]

            zeros = jnp.zeros((batch, g_cnt, n_state), F32)
            z, hr, hi = s5_scan(up, zeros, zeros, prm, batch=batch, t_len=t_len, t_valid=t_len)
            sr_p.append(hr.reshape(batch, g_cnt, n_state))
            si_p.append(hi.reshape(batch, g_cnt, n_state))
            tm, tn, tk = _mm_tiles(mp, d, d)
            hp = matmul(z, w_glu, tm=tm, tn=tn, tk=tk, epilogue="sglu", n_out=d, w2_col_off=d, res=hp)

            us_pad = jnp.pad(us[:, None, :], ((0, 0), (0, SUBLANE - ts), (0, 0))).reshape(bd * SUBLANE, d)
            z, hr, hi = s5_scan(us_pad, state_s5_re[li].astype(F32), state_s5_im[li].astype(F32), prm,
                                batch=bd, t_len=SUBLANE, t_valid=ts)
            sr_s.append(hr.reshape(bd, g_cnt, n_state))
            si_s.append(hi.reshape(bd, g_cnt, n_state))
            z_s = z.reshape(bd, SUBLANE, d)[:, 0]
            tm, tn, tk = _mm_tiles(bd, d, d)
            hs = matmul(z_s, w_glu, tm=tm, tn=tn, tk=tk, epilogue="sglu", n_out=d, w2_col_off=d, res=hs)

        hp = ffn(hp, rmsnorm(hp, norm_ffn[i], BF16), ffn_w_gate_up[i], ffn_w_down[i])
        hs = ffn(hs, rmsnorm(hs, norm_ffn[i], BF16), ffn_w_gate_up[i], ffn_w_down[i])

    y_p = rmsnorm(hp, norm_final, F32).reshape(batch, t_len, d)
    y_s = rmsnorm(hs, norm_final, F32).reshape(bd, ts, d)
    return (y_p, y_s, jnp.stack(kv_p), jnp.stack(kv_s), jnp.stack(win_p), jnp.stack(win_s),
            jnp.stack(sr_p), jnp.stack(si_p), jnp.stack(sr_s), jnp.stack(si_s))
```

```python
import functools

import jax
import jax.numpy as jnp
from jax import lax
from jax.experimental import pallas as pl
from jax.experimental.pallas import tpu as pltpu

EPS = 1e-6
NEG_INF = -1e30
ROPE_THETA = 10000.0
CMP_STRIDE = 16
CMP_LEN = 2 * CMP_STRIDE
SEL_BLOCK = 64
N_SELECT = 16
WINDOW = 512
Q_BLK = 64

LANE = 128
SUBLANE = 8
VMEM_LIMIT_BYTES = 56 * 1024 * 1024

F32 = jnp.float32
BF16 = jnp.bfloat16
_NT = (((1,), (1,)), ((), ()))


def _cparams(semantics):
    return pltpu.CompilerParams(dimension_semantics=semantics, vmem_limit_bytes=VMEM_LIMIT_BYTES)


def _tile(dim, pref, align=LANE):
    if dim <= pref:
        return dim
    t = (pref // align) * align
    while t >= align:
        if dim % t == 0:
            return t
        t -= align
    return dim


def _rmsnorm_kernel(x_ref, g_ref, o_ref):
    x = x_ref[...]
    y = x * lax.rsqrt(jnp.mean(x * x, axis=-1, keepdims=True) + EPS)
    o_ref[...] = (y * g_ref[...]).astype(o_ref.dtype)


def rmsnorm(x, g, out_dtype):
    m, d = x.shape
    tm = _tile(m, 256, SUBLANE)
    return pl.pallas_call(
        _rmsnorm_kernel,
        grid=(m // tm,),
        in_specs=[pl.BlockSpec((tm, d), lambda i: (i, 0)), pl.BlockSpec((1, d), lambda i: (0, 0))],
        out_specs=pl.BlockSpec((tm, d), lambda i: (i, 0)),
        out_shape=jax.ShapeDtypeStruct((m, d), out_dtype),
        compiler_params=_cparams(("parallel",)),
        name="rmsnorm",
    )(x, g.reshape(1, d))


def _cast_kernel(x_ref, o_ref):
    o_ref[...] = x_ref[...].astype(o_ref.dtype)


def cast_bf16(w):
    shape = w.shape
    w2 = w.reshape(-1, shape[-1])
    r, c = w2.shape
    tr = _tile(r, 256, SUBLANE)
    tc = c if c <= 8192 else _tile(c, 8192)
    out = pl.pallas_call(
        _cast_kernel,
        grid=(r // tr, c // tc),
        in_specs=[pl.BlockSpec((tr, tc), lambda i, j: (i, j))],
        out_specs=pl.BlockSpec((tr, tc), lambda i, j: (i, j)),
        out_shape=jax.ShapeDtypeStruct((r, c), BF16),
        compiler_params=_cparams(("parallel", "parallel")),
        name="cast_bf16",
    )(w2)
    return out.reshape(shape)


def _mm_kernel(*refs, tn, dual, epilogue, rope_cfg):
    it = iter(refs)
    a_ref = next(it)
    w_ref = next(it)
    w2_ref = next(it) if dual else None
    res_ref = next(it) if epilogue in ("residual", "sglu") else None
    cos_ref = next(it) if epilogue == "rope" else None
    sin_ref = next(it) if epilogue == "rope" else None
    o_ref = next(it)

    a = a_ref[...]
    acc = jnp.dot(a, w_ref[...], preferred_element_type=F32)
    if dual:
        acc2 = jnp.dot(a, w2_ref[...], preferred_element_type=F32)

    if epilogue == "residual":
        o_ref[...] = (res_ref[...] + acc).astype(o_ref.dtype)
    elif epilogue == "swiglu":
        o_ref[...] = (jax.nn.silu(acc) * acc2).astype(o_ref.dtype)
    elif epilogue == "sglu":
        o_ref[...] = (res_ref[...] + acc * jax.nn.sigmoid(acc2)).astype(o_ref.dtype)
    elif epilogue == "rope":
        q_width, kv_width, kvw, hd = rope_cfg
        col0 = pl.program_id(1) * tn
        in_kv = (col0 >= q_width) & (col0 < q_width + kv_width)
        is_rope = (col0 < q_width) | (in_kv & (((col0 - q_width) // kvw) % 2 == 0))

        @pl.when(is_rope)
        def _():
            cos = cos_ref[...]
            sin = sin_ref[...]
            for c in range(tn // hd):
                x = acc[:, c * hd:(c + 1) * hd]
                o_ref[:, c * hd:(c + 1) * hd] = x * cos + pltpu.roll(x, hd // 2, 1) * sin

        @pl.when(jnp.logical_not(is_rope))
        def _():
            o_ref[...] = acc


def matmul(a, w, *, tm, tn, epilogue, out_dtype=F32, n_out=None, w2_col_off=None, res=None, cos=None,
           sin=None, rope_cfg=None, a_buffers=2):
    m, kdim = a.shape
    n = w.shape[1] if n_out is None else n_out
    dual = epilogue in ("swiglu", "sglu")
    assert m % tm == 0
    a_spec = pl.BlockSpec((tm, kdim), lambda i, j: (i, 0))
    if a_buffers != 2:
        a_spec = pl.BlockSpec((tm, kdim), lambda i, j: (i, 0), pipeline_mode=pl.Buffered(a_buffers))
    in_specs = [a_spec, pl.BlockSpec((kdim, tn), lambda i, j: (0, j))]
    args = [a, w]
    if dual:
        assert w2_col_off % tn == 0 and n % tn == 0
        off = w2_col_off // tn
        in_specs.append(pl.BlockSpec((kdim, tn), lambda i, j: (0, j + off)))
        args.append(w)
    if epilogue in ("residual", "sglu"):
        in_specs.append(pl.BlockSpec((tm, tn), lambda i, j: (i, j)))
        args.append(res)
    if epilogue == "rope":
        hd = rope_cfg[3]
        in_specs += [pl.BlockSpec((tm, hd), lambda i, j: (i, 0))] * 2
        args += [cos, sin]
    return pl.pallas_call(
        functools.partial(_mm_kernel, tn=tn, dual=dual, epilogue=epilogue, rope_cfg=rope_cfg),
        grid=(m // tm, pl.cdiv(n, tn)),
        in_specs=in_specs,
        out_specs=pl.BlockSpec((tm, tn), lambda i, j: (i, j)),
        out_shape=jax.ShapeDtypeStruct((m, n), out_dtype),
        compiler_params=_cparams(("parallel", "arbitrary")),
        name="mm_" + epilogue,
    )(*args)


def _compress_kernel(x_ref, pe_ref, w1_ref, w2_ref, o_ref, w1_s, w2_s):
    @pl.when(pl.program_id(0) == 0)
    def _():
        w1_s[...] = w1_ref[...].astype(BF16)
        w2_s[...] = w2_ref[...].astype(BF16)

    x = (x_ref[...] + pe_ref[...]).astype(BF16)
    h = jax.nn.gelu(jnp.dot(x, w1_s[...], preferred_element_type=F32))
    o_ref[...] = jnp.dot(h.astype(BF16), w2_s[...], preferred_element_type=F32).astype(o_ref.dtype)


def compress_rows(flat, pe, w1, w2):
    r, kdim = flat.shape
    hid, hd = w2.shape
    tr = _tile(r, 512, SUBLANE)
    return pl.pallas_call(
        _compress_kernel,
        grid=(r // tr,),
        in_specs=[pl.BlockSpec((tr, kdim), lambda i: (i, 0)),
                  pl.BlockSpec((1, kdim), lambda i: (0, 0)),
                  pl.BlockSpec((kdim, hid), lambda i: (0, 0)),
                  pl.BlockSpec((hid, hd), lambda i: (0, 0))],
        out_specs=pl.BlockSpec((tr, hd), lambda i: (i, 0)),
        out_shape=jax.ShapeDtypeStruct((r, hd), BF16),
        scratch_shapes=[pltpu.VMEM((kdim, hid), BF16), pltpu.VMEM((hid, hd), BF16)],
        compiler_params=_cparams(("arbitrary",)),
        name="compress",
    )(flat, pe.reshape(1, kdim), w1, w2)


def _cmp_flat(x, ncp):
    b, tk, n_kv, hd = x.shape
    n_chunk = tk // CMP_STRIDE
    c = x[:, :n_chunk * CMP_STRIDE].reshape(b, n_chunk, CMP_STRIDE, n_kv, hd)
    blk = jnp.concatenate([c[:, :-1], c[:, 1:]], axis=2)
    flat = blk.transpose(0, 3, 1, 2, 4).reshape(b, n_kv, n_chunk - 1, CMP_LEN * hd)
    return jnp.pad(flat, ((0, 0), (0, 0), (0, ncp - (n_chunk - 1)), (0, 0)))


def _overlap_matrix(ncp, nsp):
    ic = jnp.arange(ncp, dtype=jnp.int32)[:, None] * CMP_STRIDE
    js = jnp.arange(nsp, dtype=jnp.int32)[None, :] * SEL_BLOCK
    return ((ic < js + SEL_BLOCK) & (ic + CMP_LEN > js)).astype(F32)


def _compress_paged_kernel(*refs, pg, page, hd, n_groups, n_chunk, ncp):
    pt_ref = refs[0]
    page_refs = refs[1:1 + pg]
    pe_ref, w1_ref, w2_ref, o_ref, la_s, lb_s = refs[1 + pg:]
    del pt_ref
    g = pl.program_id(3)
    cpp = page // CMP_STRIDE

    if ncp > n_chunk:
        @pl.when(g == 0)
        def _():
            la_s[...] = jnp.zeros_like(la_s)
            lb_s[...] = jnp.zeros_like(lb_s)

    for i in range(pg):
        row0 = pl.multiple_of((g * pg + i) * cpp, cpp)
        for r in range(CMP_STRIDE):
            x = page_refs[i][pl.ds(r, cpp, stride=CMP_STRIDE), :]
            la_s[pl.ds(row0, cpp), r * hd:(r + 1) * hd] = x + pe_ref[r:r + 1, :]
            lb_s[pl.ds(row0, cpp), r * hd:(r + 1) * hd] = x + pe_ref[CMP_STRIDE + r:CMP_STRIDE + r + 1, :]

    @pl.when(g == n_groups - 1)
    def _():
        half = CMP_STRIDE * hd
        w1 = w1_ref[...].astype(BF16)
        ca = jnp.dot(la_s[...].astype(BF16), w1[:half], preferred_element_type=F32)
        cb = jnp.dot(lb_s[...].astype(BF16), w1[half:], preferred_element_type=F32)
        pre = ca + pltpu.roll(cb, ncp - 1, 0)
        h = jax.nn.gelu(pre)
        o_ref[...] = jnp.dot(h.astype(BF16), w2_ref[...].astype(BF16),
                             preferred_element_type=F32).astype(o_ref.dtype)


def compress_paged(page_table, cache2d, pe, w1, w2, *, n_kv, hd, ncp):
    bd, n_pages = page_table.shape
    page = cache2d.shape[1]
    assert page % CMP_STRIDE == 0
    cpp = page // CMP_STRIDE
    n_chunk = n_pages * cpp
    assert ncp >= n_chunk and cpp == SUBLANE
    pg = _tile(n_pages, 16, 1)
    n_groups = n_pages // pg
    hid = w1.shape[-1]

    def page_spec(i):
        return pl.BlockSpec((None, page, hd),
                            lambda s, b, k, g, pt: (pt[b * n_pages + g * pg + i], 0, s * n_kv + k))

    kern = functools.partial(_compress_paged_kernel, pg=pg, page=page, hd=hd, n_groups=n_groups,
                             n_chunk=n_chunk, ncp=ncp)
    grid_spec = pltpu.PrefetchScalarGridSpec(
        num_scalar_prefetch=1,
        grid=(2, bd, n_kv, n_groups),
        in_specs=[page_spec(i) for i in range(pg)] + [
            pl.BlockSpec((None, CMP_LEN, hd), lambda s, b, k, g, pt: (s, 0, 0)),
            pl.BlockSpec((None, CMP_LEN * hd, hid), lambda s, b, k, g, pt: (s, 0, 0)),
            pl.BlockSpec((None, hid, hd), lambda s, b, k, g, pt: (s, 0, 0))],
        out_specs=pl.BlockSpec((None, None, None, ncp, hd), lambda s, b, k, g, pt: (s, b, k, 0, 0)),
        scratch_shapes=[pltpu.VMEM((ncp, CMP_STRIDE * hd), F32), pltpu.VMEM((ncp, CMP_STRIDE * hd), F32)],
    )
    return pl.pallas_call(
        kern,
        grid_spec=grid_spec,
        out_shape=jax.ShapeDtypeStruct((2, bd, n_kv, ncp, hd), BF16),
        compiler_params=_cparams(("arbitrary", "arbitrary", "arbitrary", "arbitrary")),
        name="compress_paged",
    )(page_table.reshape(-1), *([cache2d] * pg), pe, w1, w2)


def _nsa_prompt_kernel(q_ref, ks_ref, vs_ref, kw_ref, vw_ref, kc_ref, vc_ref, gate_ref, ovl_ref, exp_ref,
                       o_ref, ks_s, vs_s, kw_s, vw_s, *, gqa, hd, t_len, n_cmp, n_sel, k_sel, scale):
    qb = pl.program_id(2)

    @pl.when(qb == 0)
    def _():
        ks_s[...] = ks_ref[...].astype(BF16)
        vs_s[...] = vs_ref[...].astype(BF16)
        kw_s[...] = kw_ref[...].astype(BF16)
        vw_s[...] = vw_ref[...].astype(BF16)

    rows = gqa * Q_BLK
    b0 = qb * Q_BLK
    q = q_ref[...]
    qg = jnp.concatenate([q[:, g * hd:(g + 1) * hd] for g in range(gqa)], axis=0).astype(BF16)
    pos_q = b0 + lax.broadcasted_iota(jnp.int32, (Q_BLK, 1), 0)

    ncp = kc_ref.shape[0]
    s = lax.dot_general(qg, kc_ref[...], _NT, preferred_element_type=F32) * scale
    ci = lax.broadcasted_iota(jnp.int32, (1, ncp), 1)
    m_cmp = ((ci * CMP_STRIDE + (CMP_LEN - 1)) <= pos_q) & (ci < n_cmp)
    s3 = s.reshape(gqa, Q_BLK, ncp) + jnp.where(m_cmp, 0.0, NEG_INF)[None]
    e3 = jnp.exp(s3 - jnp.max(s3, axis=-1, keepdims=True)) * m_cmp.astype(F32)[None]
    l3 = jnp.sum(e3, axis=-1, keepdims=True)
    p3 = e3 * jnp.where(l3 > 0.0, 1.0 / l3, 0.0)
    o_cmp = jnp.dot(p3.reshape(rows, ncp).astype(BF16), vc_ref[...], preferred_element_type=F32)

    nsp = ovl_ref.shape[1]
    imp = jnp.dot(jnp.sum(p3, axis=0), ovl_ref[...], preferred_element_type=F32,
                  precision=lax.Precision.HIGHEST)
    jj = lax.broadcasted_iota(jnp.int32, (1, nsp), 1)
    cur = pos_q // SEL_BLOCK
    forced = (jj == 0) | (jj == cur) | (jj == cur - 1)
    future = jj * SEL_BLOCK > pos_q
    score = jnp.where(future, -jnp.inf, jnp.where(forced, jnp.inf, imp))
    rank = jnp.zeros((Q_BLK, nsp), F32)
    for j2 in range(n_sel):
        col = score[:, j2:j2 + 1]
        beats = (col > score) | ((col == score) & (j2 < jj))
        rank = rank + beats.astype(F32)
    sel = ((rank < k_sel) & (jj < n_sel)).astype(BF16)
    selk = jnp.dot(sel, exp_ref[...], preferred_element_type=F32)

    s = lax.dot_general(qg, ks_s[...], _NT, preferred_element_type=F32) * scale
    kpos = lax.broadcasted_iota(jnp.int32, (1, t_len), 1)
    m_sel = (selk > 0.5) & (kpos <= pos_q)
    s3 = s.reshape(gqa, Q_BLK, t_len) + jnp.where(m_sel, 0.0, NEG_INF)[None]
    e3 = jnp.exp(s3 - jnp.max(s3, axis=-1, keepdims=True))
    l_sel = jnp.sum(e3, axis=-1, keepdims=True).reshape(rows, 1)
    o_sel = jnp.dot(e3.reshape(rows, t_len).astype(BF16), vs_s[...], preferred_element_type=F32) / l_sel

    wk = WINDOW + Q_BLK
    start = pl.multiple_of(jnp.maximum(b0 - WINDOW, 0), Q_BLK)
    s = lax.dot_general(qg, kw_s[pl.ds(start, wk), :], _NT, preferred_element_type=F32) * scale
    dist = pos_q - (start + lax.broadcasted_iota(jnp.int32, (1, wk), 1))
    m_win = (dist >= 0) & (dist <= WINDOW)
    s3 = s.reshape(gqa, Q_BLK, wk) + jnp.where(m_win, 0.0, NEG_INF)[None]
    e3 = jnp.exp(s3 - jnp.max(s3, axis=-1, keepdims=True))
    l_win = jnp.sum(e3, axis=-1, keepdims=True).reshape(rows, 1)
    o_win = jnp.dot(e3.reshape(rows, wk).astype(BF16), vw_s[pl.ds(start, wk), :],
                    preferred_element_type=F32) / l_win

    gate = jax.nn.sigmoid(gate_ref[...])
    for g in range(gqa):
        r = slice(g * Q_BLK, (g + 1) * Q_BLK)
        o = (gate[:, 3 * g:3 * g + 1] * o_cmp[r] + gate[:, 3 * g + 1:3 * g + 2] * o_sel[r]
             + gate[:, 3 * g + 2:3 * g + 3] * o_win[r])
        o_ref[:, g * hd:(g + 1) * hd] = o.astype(o_ref.dtype)


def nsa_prompt_attention(proj, gates_r, kc, vc, *, batch, t_len, n_kv, gqa, hd, n_cmp):
    m = batch * t_len
    nq = t_len // Q_BLK
    q_width = n_kv * gqa * hd
    kvw = n_kv * hd
    ncp = kc.shape[2]
    n_sel = -(-t_len // SEL_BLOCK)
    nsp = -(-n_sel // LANE) * LANE
    k_sel = min(N_SELECT, n_sel)
    assert t_len % Q_BLK == 0 and t_len >= WINDOW + Q_BLK and t_len % SEL_BLOCK == 0
    ovl = _overlap_matrix(ncp, nsp)
    expand = (jnp.arange(nsp, dtype=jnp.int32)[:, None]
              == jnp.arange(t_len, dtype=jnp.int32)[None, :] // SEL_BLOCK).astype(BF16)

    def kv_spec(slot):
        base = (q_width + slot * kvw) // hd
        return pl.BlockSpec((t_len, hd), lambda b, k, i: (b, base + k))

    kern = functools.partial(_nsa_prompt_kernel, gqa=gqa, hd=hd, t_len=t_len, n_cmp=n_cmp, n_sel=n_sel,
                             k_sel=k_sel, scale=hd ** -0.5)
    return pl.pallas_call(
        kern,
        grid=(batch, n_kv, nq),
        in_specs=[pl.BlockSpec((Q_BLK, gqa * hd), lambda b, k, i: (b * nq + i, k)),
                  kv_spec(2), kv_spec(3), kv_spec(4), kv_spec(5),
                  pl.BlockSpec((None, None, ncp, hd), lambda b, k, i: (b, k, 0, 0)),
                  pl.BlockSpec((None, None, ncp, hd), lambda b, k, i: (b, k, 0, 0)),
                  pl.BlockSpec((None, Q_BLK, 3 * gqa), lambda b, k, i: (k, b * nq + i, 0)),
                  pl.BlockSpec((ncp, nsp), lambda b, k, i: (0, 0)),
                  pl.BlockSpec((nsp, t_len), lambda b, k, i: (0, 0))],
        out_specs=pl.BlockSpec((Q_BLK, gqa * hd), lambda b, k, i: (b * nq + i, k)),
        out_shape=jax.ShapeDtypeStruct((m, q_width), BF16),
        scratch_shapes=[pltpu.VMEM((t_len, hd), BF16)] * 4,
        compiler_params=_cparams(("parallel", "parallel", "arbitrary")),
        name="nsa_prompt",
    )(proj, proj, proj, proj, proj, kc, vc, gates_r, ovl, expand)


def _nsa_dec_select_kernel(q_ref, kc_ref, vc_ref, ovl_ref, ocmp_ref, idx_ref, *, n_cmp, n_sel, pos_q, scale):
    q = q_ref[...].astype(BF16)
    ncp = kc_ref.shape[0]
    nsp = ovl_ref.shape[1]
    s = lax.dot_general(q, kc_ref[...], _NT, preferred_element_type=F32) * scale
    ci = lax.broadcasted_iota(jnp.int32, (1, ncp), 1)
    m_cmp = ((ci * CMP_STRIDE + (CMP_LEN - 1)) <= pos_q) & (ci < n_cmp)
    s = jnp.where(m_cmp, s, NEG_INF)
    e = jnp.where(m_cmp, jnp.exp(s - jnp.max(s, axis=-1, keepdims=True)), 0.0)
    l = jnp.sum(e, axis=-1, keepdims=True)
    p = e * jnp.where(l > 0.0, 1.0 / l, 0.0)
    ocmp_ref[...] = jnp.dot(p.astype(BF16), vc_ref[...], preferred_element_type=F32)

    imp = jnp.sum(jnp.dot(p, ovl_ref[...], preferred_element_type=F32, precision=lax.Precision.HIGHEST),
                  axis=0, keepdims=True)
    jj = lax.broadcasted_iota(jnp.int32, (1, nsp), 1)
    cur = pos_q // SEL_BLOCK
    forced = (jj == 0) | (jj == cur) | (jj == cur - 1)
    future = jj * SEL_BLOCK > pos_q
    score = jnp.where(future, -jnp.inf, jnp.where(forced, jnp.inf, imp))
    rr = lax.broadcasted_iota(jnp.int32, (nsp, nsp), 0)
    cc = lax.broadcasted_iota(jnp.int32, (nsp, nsp), 1)
    score_b = jnp.broadcast_to(score, (nsp, nsp))
    score_col = jnp.sum(jnp.where(rr == cc, score_b, 0.0), axis=1, keepdims=True)
    beaten = ((score_b > score_col) | ((score_b == score_col) & (cc < rr))) & (cc < n_sel)
    rank_col = jnp.sum(beaten.astype(F32), axis=1, keepdims=True)
    lane = lax.broadcasted_iota(jnp.int32, (nsp, LANE), 1)
    row = lax.broadcasted_iota(jnp.int32, (nsp, LANE), 0)
    hit = (rank_col == lane.astype(F32)) & (row < n_sel)
    idx_ref[...] = jnp.sum(jnp.where(hit, row.astype(F32), 0.0), axis=0, keepdims=True).astype(jnp.int32)


def nsa_dec_select(q, kc, vc, *, n_cmp, n_sel, pos_q):
    bd, n_kv, gqa, hd = q.shape
    ncp = kc.shape[2]
    nsp = -(-n_sel // LANE) * LANE
    kern = functools.partial(_nsa_dec_select_kernel, n_cmp=n_cmp, n_sel=n_sel, pos_q=pos_q, scale=hd ** -0.5)
    return pl.pallas_call(
        kern,
        grid=(bd, n_kv),
        in_specs=[pl.BlockSpec((None, None, gqa, hd), lambda b, k: (b, k, 0, 0)),
                  pl.BlockSpec((None, None, ncp, hd), lambda b, k: (b, k, 0, 0)),
                  pl.BlockSpec((None, None, ncp, hd), lambda b, k: (b, k, 0, 0)),
                  pl.BlockSpec((ncp, nsp), lambda b, k: (0, 0))],
        out_specs=[pl.BlockSpec((None, None, gqa, hd), lambda b, k: (b, k, 0, 0)),
                   pl.BlockSpec((None, None, 1, LANE), lambda b, k: (b, k, 0, 0))],
        out_shape=[jax.ShapeDtypeStruct((bd, n_kv, gqa, hd), F32),
                   jax.ShapeDtypeStruct((bd, n_kv, 1, LANE), jnp.int32)],
        compiler_params=_cparams(("parallel", "parallel")),
        name="nsa_dec_select",
    )(q, kc, vc, _overlap_matrix(ncp, nsp))


def _nsa_dec_attend_kernel(*refs, k_sel, n_kv, past_len, pos_q, scale):
    pt_ref, idx_ref, q_ref = refs[:3]
    k_refs = refs[3:3 + k_sel]
    v_refs = refs[3 + k_sel:3 + 2 * k_sel]
    new_ref, kw_ref, vw_ref, ocmp_ref, gate_ref, o_ref = refs[3 + 2 * k_sel:]
    del pt_ref
    b = pl.program_id(0)
    k = pl.program_id(1)

    q = q_ref[...].astype(BF16)
    new = new_ref[...].astype(BF16)
    newf = new.astype(F32)
    s_all = lax.dot_general(q, new, _NT, preferred_element_type=F32) * scale

    first = (lax.broadcasted_iota(jnp.int32, (SEL_BLOCK, 1), 0) == 0).astype(F32)
    new_k = first * new_ref[2:3, :]
    new_v = first * new_ref[3:4, :]
    off = lax.broadcasted_iota(jnp.int32, (1, SEL_BLOCK), 1)
    s_blk, v_blk = [], []
    for r in range(k_sel):
        tok0 = idx_ref[(b * n_kv + k) * k_sel + r] * SEL_BLOCK
        is_new = tok0 >= past_len
        kb = jnp.where(is_new, new_k, k_refs[r][...]).astype(BF16)
        v_blk.append(jnp.where(is_new, new_v, v_refs[r][...]).astype(BF16))
        s = lax.dot_general(q, kb, _NT, preferred_element_type=F32) * scale
        s_blk.append(jnp.where(tok0 + off <= pos_q, s, NEG_INF))
    mx = functools.reduce(jnp.maximum, [jnp.max(s, axis=-1, keepdims=True) for s in s_blk])
    l_sel = jnp.zeros_like(mx)
    acc = jnp.zeros((q.shape[0], q.shape[1]), F32)
    for s, vb in zip(s_blk, v_blk):
        e = jnp.exp(s - mx)
        l_sel = l_sel + jnp.sum(e, axis=-1, keepdims=True)
        acc = acc + jnp.dot(e.astype(BF16), vb, preferred_element_type=F32)
    o_sel = acc / l_sel

    wb = kw_ref.shape[0]
    s = lax.dot_general(q, kw_ref[...].astype(BF16), _NT, preferred_element_type=F32) * scale
    pos_w = (past_len - wb) + lax.broadcasted_iota(jnp.int32, (1, wb), 1)
    dist = pos_q - pos_w
    m_win = (dist >= 0) & (dist <= WINDOW) & (pos_w >= 0)
    s = jnp.where(m_win, s, NEG_INF)
    s_nw = s_all[:, 4:5]
    mx = jnp.maximum(jnp.max(s, axis=-1, keepdims=True), s_nw)
    e = jnp.where(m_win, jnp.exp(s - mx), 0.0)
    e_nw = jnp.exp(s_nw - mx)
    l_w = jnp.sum(e, axis=-1, keepdims=True) + e_nw
    o_win = (jnp.dot(e.astype(BF16), vw_ref[...].astype(BF16), preferred_element_type=F32)
             + e_nw.astype(BF16).astype(F32) * newf[5:6, :]) / l_w

    gate = jax.nn.sigmoid(gate_ref[...])
    o = gate[:, 0:1] * ocmp_ref[...] + gate[:, 1:2] * o_sel + gate[:, 2:3] * o_win
    o_ref[...] = o.astype(o_ref.dtype)


def nsa_dec_attend(page_table, idx, q, cache2d, new_rows, win2d, o_cmp, gates, *, k_sel, past_len, pos_q):
    bd, n_kv, gqa, hd = q.shape
    n_pages = page_table.shape[1]
    page = cache2d.shape[1]
    wb = win2d.shape[1]
    kvw = n_kv * hd
    assert page % SEL_BLOCK == 0 and past_len % SEL_BLOCK == 0 and pos_q == past_len
    bpp = page // SEL_BLOCK
    kern = functools.partial(_nsa_dec_attend_kernel, k_sel=k_sel, n_kv=n_kv, past_len=past_len, pos_q=pos_q,
                             scale=hd ** -0.5)

    def blk_spec(slot, r):
        base = slot * kvw // hd

        def index_map(b, k, pt, ix):
            j = ix[(b * n_kv + k) * k_sel + r]
            return pt[b * n_pages + jnp.minimum(j // bpp, n_pages - 1)], j % bpp, base + k

        return pl.BlockSpec((None, SEL_BLOCK, hd), index_map)

    grid_spec = pltpu.PrefetchScalarGridSpec(
        num_scalar_prefetch=2,
        grid=(bd, n_kv),
        in_specs=[pl.BlockSpec((None, None, gqa, hd), lambda b, k, pt, ix: (b, k, 0, 0))]
        + [blk_spec(2, r) for r in range(k_sel)] + [blk_spec(3, r) for r in range(k_sel)]
        + [pl.BlockSpec((None, None, SUBLANE, hd), lambda b, k, pt, ix: (b, k, 0, 0)),
           pl.BlockSpec((None, wb, hd), lambda b, k, pt, ix: (b, 0, k)),
           pl.BlockSpec((None, wb, hd), lambda b, k, pt, ix: (b, 0, n_kv + k)),
           pl.BlockSpec((None, None, gqa, hd), lambda b, k, pt, ix: (b, k, 0, 0)),
           pl.BlockSpec((None, None, gqa, 3), lambda b, k, pt, ix: (b, k, 0, 0))],
        out_specs=pl.BlockSpec((None, None, gqa, hd), lambda b, k, pt, ix: (b, k, 0, 0)),
    )
    return pl.pallas_call(
        kern,
        grid_spec=grid_spec,
        out_shape=jax.ShapeDtypeStruct((bd, n_kv, gqa, hd), BF16),
        compiler_params=_cparams(("arbitrary", "arbitrary")),
        name="nsa_dec_attend",
    )(page_table.reshape(-1), idx, q, *([cache2d] * (2 * k_sel)), new_rows, win2d, win2d, o_cmp, gates)


N_POW = SUBLANE


def _s5_disc_kernel(lre_ref, lim_ref, ldt_ref, bre_ref, bim_ref, bbre_ref, bbim_ref, pre_ref, pim_ref):
    lam_re = lre_ref[...]
    lam_im = lim_ref[...]
    dt = jnp.exp(ldt_ref[...])
    mag = jnp.exp(lam_re * dt)
    ab_re = mag * jnp.cos(lam_im * dt)
    ab_im = mag * jnp.sin(lam_im * dt)
    nr, ni = ab_re - 1.0, ab_im
    den = lam_re * lam_re + lam_im * lam_im
    f_re = (nr * lam_re + ni * lam_im) / den
    f_im = (ni * lam_re - nr * lam_im) / den
    b_re = bre_ref[...]
    b_im = bim_ref[...]
    bbre_ref[...] = f_re * b_re - f_im * b_im
    bbim_ref[...] = f_re * b_im + f_im * b_re
    pr, pi = ab_re, ab_im
    pre_ref[0] = pr
    pim_ref[0] = pi
    for n in range(1, N_POW):
        pr, pi = pr * ab_re - pi * ab_im, pr * ab_im + pi * ab_re
        pre_ref[n] = pr
        pim_ref[n] = pi


def s5_discretize(lam_re, lam_im, log_dt, b_re, b_im):
    g, n, c = b_re.shape
    rep = lambda x: jnp.broadcast_to(x[:, None, :], (g, c, n)).reshape(g * c, n)
    ldt = jnp.broadcast_to(log_dt[:, None, None], (g, c, n)).reshape(g * c, n)
    bt = lambda x: x.transpose(0, 2, 1).reshape(g * c, n)
    full = pl.BlockSpec((g * c, n), lambda: (0, 0))
    pw = pl.BlockSpec((N_POW, g * c, n), lambda: (0, 0, 0))
    bbre, bbim, pre, pim = pl.pallas_call(
        _s5_disc_kernel,
        in_specs=[full] * 5,
        out_specs=[full, full, pw, pw],
        out_shape=[jax.ShapeDtypeStruct((g * c, n), F32)] * 2
        + [jax.ShapeDtypeStruct((N_POW, g * c, n), F32)] * 2,
        compiler_params=pltpu.CompilerParams(vmem_limit_bytes=VMEM_LIMIT_BYTES),
        name="s5_discretize",
    )(rep(lam_re), rep(lam_im), ldt, bt(b_re), bt(b_im))
    pick = lambda x: x.reshape(N_POW, g, c, n)[:, :, 0, :]
    return bbre.reshape(g, c, n), bbim.reshape(g, c, n), pick(pre), pick(pim)


def _s5_scan_kernel(u_ref, h0r_ref, h0i_ref, bbr_ref, bbi_ref, ccr_ref, cci_ref, d_ref, mr_ref, mi_ref,
                    pr_ref, pi_ref, z_ref, hr_ref, hi_ref, xr_s, xi_s, cr_s, ci_s, *, tt, n_t, last_row):
    t = pl.program_id(2)

    @pl.when(t == 0)
    def _():
        cr_s[...] = h0r_ref[...]
        ci_s[...] = h0i_ref[...]

    u = u_ref[...]
    ub = u.astype(BF16)
    xr_s[...] = jnp.dot(ub, bbr_ref[...].astype(BF16), preferred_element_type=F32)
    xi_s[...] = jnp.dot(ub, bbi_ref[...].astype(BF16), preferred_element_type=F32)
    pr = pr_ref[...]
    pi = pi_ref[...]

    def group(r, carry):
        cr, ci = carry
        rows = pl.ds(pl.multiple_of(r * SUBLANE, SUBLANE), SUBLANE)
        xr = xr_s[rows, :]
        xi = xi_s[rows, :]
        for n, sh in enumerate((1, 2, 4)):
            mr = mr_ref[n]
            mi = mi_ref[n]
            sr = pltpu.roll(xr, sh, 0)
            si = pltpu.roll(xi, sh, 0)
            xr, xi = xr + (mr * sr - mi * si), xi + (mr * si + mi * sr)
        xr, xi = xr + (pr * cr - pi * ci), xi + (pr * ci + pi * cr)
        xr_s[rows, :] = xr
        xi_s[rows, :] = xi
        return xr[SUBLANE - 1:SUBLANE, :], xi[SUBLANE - 1:SUBLANE, :]

    cr, ci = lax.fori_loop(0, tt // SUBLANE, group, (cr_s[...], ci_s[...]))
    cr_s[...] = cr
    ci_s[...] = ci

    y = (jnp.dot(xr_s[...].astype(BF16), ccr_ref[...].astype(BF16), preferred_element_type=F32)
         - jnp.dot(xi_s[...].astype(BF16), cci_ref[...].astype(BF16), preferred_element_type=F32)
         + d_ref[...] * u)
    z_ref[...] = jax.nn.gelu(y).astype(z_ref.dtype)

    @pl.when(t == n_t - 1)
    def _():
        hr_ref[...] = xr_s[last_row:last_row + 1, :]
        hi_ref[...] = xi_s[last_row:last_row + 1, :]


def s5_scan(u, h0_re, h0_im, prm, *, batch, t_len, t_valid):
    bb_re, bb_im, cc_re, cc_im, d_skip, m_re, m_im, p_re, p_im = prm
    n_slab, cw, sw = bb_re.shape
    tt = _tile(t_len, 256, SUBLANE)
    n_t = t_len // tt
    assert t_valid > (n_t - 1) * tt
    last_row = (t_valid - 1) % tt
    h0r = h0_re.reshape(batch, n_slab, 1, sw)
    h0i = h0_im.reshape(batch, n_slab, 1, sw)
    slab3 = lambda r, c: pl.BlockSpec((None, r, c), lambda b, s, t: (s, 0, 0))
    st_spec = pl.BlockSpec((None, None, 1, sw), lambda b, s, t: (b, s, 0, 0))
    tab4 = lambda r: pl.BlockSpec((None, r, SUBLANE, sw), lambda b, s, t: (s, 0, 0, 0))
    kern = functools.partial(_s5_scan_kernel, tt=tt, n_t=n_t, last_row=last_row)
    z, hr, hi = pl.pallas_call(
        kern,
        grid=(batch, n_slab, n_t),
        in_specs=[pl.BlockSpec((tt, cw), lambda b, s, t: (b * n_t + t, s)),
                  st_spec, st_spec,
                  slab3(cw, sw), slab3(cw, sw), slab3(sw, cw), slab3(sw, cw),
                  pl.BlockSpec((1, cw), lambda b, s, t: (0, s)),
                  tab4(3), tab4(3),
                  slab3(SUBLANE, sw), slab3(SUBLANE, sw)],
        out_specs=[pl.BlockSpec((tt, cw), lambda b, s, t: (b * n_t + t, s)), st_spec, st_spec],
        out_shape=[jax.ShapeDtypeStruct((batch * t_len, n_slab * cw), BF16),
                   jax.ShapeDtypeStruct((batch, n_slab, 1, sw), F32),
                   jax.ShapeDtypeStruct((batch, n_slab, 1, sw), F32)],
        scratch_shapes=[pltpu.VMEM((tt, sw), F32), pltpu.VMEM((tt, sw), F32),
                        pltpu.VMEM((1, sw), F32), pltpu.VMEM((1, sw), F32)],
        compiler_params=_cparams(("parallel", "parallel", "arbitrary")),
        name="s5_scan",
    )(u, h0r, h0i, bb_re, bb_im, cc_re, cc_im, d_skip.reshape(1, -1), m_re, m_im, p_re, p_im)
    return z, hr, hi


def s5_prepare(lam_re, lam_im, log_dt, b_re, b_im, c_re, c_im, d_skip):
    g, n, c = b_re.shape
    gps = LANE // c
    n_slab = g // gps
    sw = gps * n
    bbt_re, bbt_im, pw_re, pw_im = s5_discretize(lam_re, lam_im, log_dt, b_re, b_im)
    eye = jnp.eye(gps, dtype=F32)

    def blockdiag_in(bt):
        x = bt.reshape(n_slab, gps, c, 1, n) * eye[None, :, None, :, None]
        return x.reshape(n_slab, gps * c, sw)

    def blockdiag_out(cm):
        x = cm.transpose(0, 2, 1).reshape(n_slab, gps, n, 1, c) * eye[None, :, None, :, None]
        return x.reshape(n_slab, sw, gps * c)

    pw_re = pw_re.reshape(N_POW, n_slab, sw)
    pw_im = pw_im.reshape(N_POW, n_slab, sw)
    row = jnp.arange(SUBLANE)[None, :, None]

    def shift_tables(pw):
        tabs = [jnp.where(row >= sh, pw[sh - 1][:, None, :], 0.0) for sh in (1, 2, 4)]
        return jnp.stack(tabs, axis=1)

    carry = lambda pw: pw.transpose(1, 0, 2)
    return (blockdiag_in(bbt_re), blockdiag_in(bbt_im), blockdiag_out(c_re.astype(F32)),
            blockdiag_out(c_im.astype(F32)), d_skip.astype(F32), shift_tables(pw_re), shift_tables(pw_im),
            carry(pw_re), carry(pw_im))


MM_ROWS = 1024


def _rope_tables(pos, hd):
    half = hd // 2
    inv = ROPE_THETA ** (-jnp.arange(half, dtype=F32) / half)
    ang = pos.astype(F32)[:, None] * inv[None, :]
    cos = jnp.cos(ang)
    sin = jnp.sin(ang)
    return jnp.concatenate([cos, cos], axis=-1), jnp.concatenate([-sin, sin], axis=-1)


def in_proj(u, w_in, pos, *, q_width, kv_width, kvw, hd):
    m = u.shape[0]
    cos, sin = _rope_tables(pos, hd)
    return matmul(u, w_in, tm=_tile(m, MM_ROWS, SUBLANE), tn=kvw, epilogue="rope", cos=cos, sin=sin,
                  rope_cfg=(q_width, kv_width, kvw, hd))


def out_proj(o, w, res):
    m, d = res.shape
    return matmul(o, w, tm=_tile(m, MM_ROWS, SUBLANE), tn=_tile(d, 512), epilogue="residual", res=res)


def glu_proj(z, w_glu, res):
    m, d = res.shape
    return matmul(z, w_glu, tm=_tile(m, MM_ROWS, SUBLANE), tn=_tile(d, 512), epilogue="sglu", n_out=d,
                  w2_col_off=d, res=res)


def ffn(h, hn, w_gu, w_down):
    m, d = h.shape
    d_ff = w_down.shape[0]
    tm = _tile(m, MM_ROWS, SUBLANE)
    act = matmul(hn, w_gu, tm=tm, tn=_tile(d_ff, 512), epilogue="swiglu", out_dtype=BF16, n_out=d_ff,
                 w2_col_off=d_ff)
    return matmul(act, w_down, tm=tm, tn=_tile(d, 256), epilogue="residual", res=h, a_buffers=1)


def kernel(x_prompt, x_sample, cache_nsa, cache_win, state_s5_re, state_s5_im, page_table, norm_mix, norm_ffn,
           norm_final, att_w_in, att_w_out, cmp_pe, cmp_w1, cmp_w2, s5_lambda_re, s5_lambda_im, s5_log_dt,
           s5_b_re, s5_b_im, s5_c_re, s5_c_im, s5_d, s5_w_glu, ffn_w_gate_up, ffn_w_down):
    batch, t_len, d = x_prompt.shape
    bd, ts, _ = x_sample.shape
    _, n_pool, page, _, n_kv, hd = cache_nsa.shape
    wb = cache_win.shape[2]
    n_pages = page_table.shape[1]
    past_len = n_pages * page
    depth = norm_mix.shape[0]
    n_heads = d // hd
    gqa = n_heads // n_kv
    q_width = n_heads * hd
    kvw = n_kv * hd
    kv_width = 6 * kvw
    assert ts == 1 and hd == LANE and wb == min(WINDOW, past_len)
    mp = batch * t_len

    att_w_in_b, att_w_out_b = cast_bf16(att_w_in), cast_bf16(att_w_out)
    s5_w_glu_b = cast_bf16(s5_w_glu)
    ffn_w_gu_b, ffn_w_down_b = cast_bf16(ffn_w_gate_up), cast_bf16(ffn_w_down)

    hp = x_prompt.reshape(mp, d)
    hs = x_sample.reshape(bd, d)
    kv_p, kv_s, win_p, win_s = [], [], [], []
    sr_p, si_p, sr_s, si_s = [], [], [], []
    pos_p = jnp.tile(jnp.arange(t_len, dtype=jnp.int32), batch)
    pos_s = jnp.full((bd,), past_len, jnp.int32)

    for i in range(depth):
        li = i // 2
        if i % 2 == 0:
            up = rmsnorm(hp, norm_mix[i], BF16)
            us = rmsnorm(hs, norm_mix[i], BF16)
            w_in, w_out, pe, w1, w2 = att_w_in_b[li], att_w_out_b[li], cmp_pe[li], cmp_w1[li], cmp_w2[li]
            cfg = dict(q_width=q_width, kv_width=kv_width, kvw=kvw, hd=hd)

            proj = in_proj(up, w_in, pos_p, **cfg)
            proj3 = proj.reshape(batch, t_len, -1)
            kv_p.append(proj3[:, :, q_width:q_width + 4 * kvw].reshape(batch, t_len, 4, n_kv, hd))
            w_keep = min(WINDOW, t_len)
            win_p.append(proj3[:, t_len - w_keep:, q_width + 4 * kvw:q_width + 6 * kvw]
                         .reshape(batch, w_keep, 2, n_kv, hd))
            n_cmp = t_len // CMP_STRIDE - 1
            ncp = -(-n_cmp // LANE) * LANE
            kc, vc = [
                compress_rows(
                    _cmp_flat(proj3[:, :, q_width + s * kvw:q_width + (s + 1) * kvw]
                              .reshape(batch, t_len, n_kv, hd), ncp).reshape(batch * n_kv * ncp, CMP_LEN * hd),
                    pe[s], w1[s], w2[s]).reshape(batch, n_kv, ncp, hd)
                for s in (0, 1)]
            gates_r = proj[:, q_width + kv_width:].reshape(mp, n_kv, 3 * gqa).transpose(1, 0, 2)
            o = nsa_prompt_attention(proj, gates_r, kc, vc, batch=batch, t_len=t_len, n_kv=n_kv, gqa=gqa,
                                     hd=hd, n_cmp=n_cmp)
            hp = out_proj(o, w_out, hp)

            proj_s = in_proj(us, w_in, pos_s, **cfg)
            rows_s = proj_s[:, q_width:q_width + kv_width].reshape(bd, 6, n_kv, hd)
            kv_s.append(rows_s[:, None, :4])
            win_s.append(jnp.concatenate([cache_win[li][:, ts:], rows_s[:, None, 4:]], axis=1))
            tk_len = past_len + ts
            assert past_len % CMP_STRIDE == 0 and ts < CMP_STRIDE
            n_cmp_s = tk_len // CMP_STRIDE - 1
            ncp_s = -(-n_cmp_s // LANE) * LANE
            cache2d = cache_nsa[li].reshape(n_pool, page, 4 * kvw)
            kvc = compress_paged(page_table, cache2d, pe, w1, w2, n_kv=n_kv, hd=hd, ncp=ncp_s)
            q_s = proj_s[:, :q_width].reshape(bd, n_kv, gqa, hd)
            n_sel_s = -(-tk_len // SEL_BLOCK)
            k_sel_s = min(N_SELECT, n_sel_s)
            o_cmp, idx = nsa_dec_select(q_s, kvc[0], kvc[1], n_cmp=n_cmp_s, n_sel=n_sel_s, pos_q=past_len)
            new_rows = jnp.pad(rows_s.transpose(0, 2, 1, 3), ((0, 0), (0, 0), (0, SUBLANE - 6), (0, 0)))
            gates_s = proj_s[:, q_width + kv_width:].reshape(bd, n_kv, gqa, 3)
            o_s = nsa_dec_attend(page_table, idx[:, :, 0, :k_sel_s].reshape(-1), q_s, cache2d, new_rows,
                                 cache_win[li].reshape(bd, wb, 2 * kvw), o_cmp, gates_s, k_sel=k_sel_s,
                                 past_len=past_len, pos_q=past_len)
            hs = out_proj(o_s.reshape(bd, q_width), w_out, hs)
        else:
            up = rmsnorm(hp, norm_mix[i], F32)
            us = rmsnorm(hs, norm_mix[i], F32)
            prm = s5_prepare(s5_lambda_re[li], s5_lambda_im[li], s5_log_dt[li], s5_b_re[li], s5_b_im[li],
                             s5_c_re[li], s5_c_im[li], s5_d[li])
            g_cnt, n_state = s5_lambda_re.shape[1:]
            w_glu = s5_w_glu_b[li]

            zeros = jnp.zeros((batch, g_cnt, n_state), F32)
            z, hr, hi = s5_scan(up, zeros, zeros, prm, batch=batch, t_len=t_len, t_valid=t_len)
            sr_p.append(hr.reshape(batch, g_cnt, n_state))
            si_p.append(hi.reshape(batch, g_cnt, n_state))
            hp = glu_proj(z, w_glu, hp)

            us_pad = jnp.pad(us[:, None, :], ((0, 0), (0, SUBLANE - ts), (0, 0))).reshape(bd * SUBLANE, d)
            z, hr, hi = s5_scan(us_pad, state_s5_re[li].astype(F32), state_s5_im[li].astype(F32), prm,
                                batch=bd, t_len=SUBLANE, t_valid=ts)
            sr_s.append(hr.reshape(bd, g_cnt, n_state))
            si_s.append(hi.reshape(bd, g_cnt, n_state))
            hs = glu_proj(z.reshape(bd, SUBLANE, d)[:, 0], w_glu, hs)

        hp = ffn(hp, rmsnorm(hp, norm_ffn[i], BF16), ffn_w_gu_b[i], ffn_w_down_b[i])
        hs = ffn(hs, rmsnorm(hs, norm_ffn[i], BF16), ffn_w_gu_b[i], ffn_w_down_b[i])

    y_p = rmsnorm(hp, norm_final, F32).reshape(batch, t_len, d)
    y_s = rmsnorm(hs, norm_final, F32).reshape(bd, ts, d)
    return (y_p, y_s, jnp.stack(kv_p), jnp.stack(kv_s), jnp.stack(win_p), jnp.stack(win_s),
            jnp.stack(sr_p), jnp.stack(si_p), jnp.stack(sr_s), jnp.stack(si_s))
```

```python
import functools

import jax
import jax.numpy as jnp
from jax import lax
from jax.experimental import pallas as pl
from jax.experimental.pallas import tpu as pltpu

EPS = 1e-6
NEG_INF = -1e30
ROPE_THETA = 10000.0
CMP_STRIDE = 16
CMP_LEN = 2 * CMP_STRIDE
SEL_BLOCK = 64
N_SELECT = 16
WINDOW = 512
Q_BLK = 64

LANE = 128
SUBLANE = 8
VMEM_LIMIT_BYTES = 56 * 1024 * 1024

F32 = jnp.float32
BF16 = jnp.bfloat16
_NT = (((1,), (1,)), ((), ()))
LOG2_E = 1.4426950408889634


def _cparams(semantics):
    return pltpu.CompilerParams(dimension_semantics=semantics, vmem_limit_bytes=VMEM_LIMIT_BYTES)


def _tile(dim, pref, align=LANE):
    if dim <= pref:
        return dim
    t = (pref // align) * align
    while t >= align:
        if dim % t == 0:
            return t
        t -= align
    return dim


def _rmsnorm_kernel(x_ref, g_ref, o_ref):
    x = x_ref[...]
    y = x * lax.rsqrt(jnp.mean(x * x, axis=-1, keepdims=True) + EPS)
    o_ref[...] = (y * g_ref[...]).astype(o_ref.dtype)


def rmsnorm(x, g, out_dtype, time_major=None):
    m, d = x.shape
    if time_major is None:
        tm = _tile(m, 256, SUBLANE)
        out_spec = pl.BlockSpec((tm, d), lambda i: (i, 0))
        out_shape = (m, d)
    else:
        batch, t_len = time_major
        tm = _tile(t_len, 256, SUBLANE)
        nt = t_len // tm
        out_spec = pl.BlockSpec((tm, d), lambda i: (i % nt, i // nt))
        out_shape = (t_len, batch * d)
    out = pl.pallas_call(
        _rmsnorm_kernel,
        grid=(m // tm,),
        in_specs=[pl.BlockSpec((tm, d), lambda i: (i, 0)), pl.BlockSpec((1, d), lambda i: (0, 0))],
        out_specs=out_spec,
        out_shape=jax.ShapeDtypeStruct(out_shape, out_dtype),
        compiler_params=_cparams(("parallel",)),
        name="rmsnorm",
    )(x, g.reshape(1, d))
    return out.reshape(m, d)


def _cast_kernel(x_ref, o_ref):
    o_ref[...] = x_ref[...].astype(o_ref.dtype)


def cast_bf16(w):
    shape = w.shape
    w2 = w.reshape(-1, shape[-1])
    r, c = w2.shape
    tr = _tile(r, 256, SUBLANE)
    tc = c if c <= 8192 else _tile(c, 8192)
    out = pl.pallas_call(
        _cast_kernel,
        grid=(r // tr, c // tc),
        in_specs=[pl.BlockSpec((tr, tc), lambda i, j: (i, j))],
        out_specs=pl.BlockSpec((tr, tc), lambda i, j: (i, j)),
        out_shape=jax.ShapeDtypeStruct((r, c), BF16),
        compiler_params=_cparams(("parallel", "parallel")),
        name="cast_bf16",
    )(w2)
    return out.reshape(shape)


def _mm_kernel(*refs, tn, dual, epilogue, rope_cfg):
    it = iter(refs)
    a_ref = next(it)
    w_ref = next(it)
    w2_ref = next(it) if dual else None
    res_ref = next(it) if epilogue in ("residual", "sglu") else None
    cos_ref = next(it) if epilogue == "rope" else None
    sin_ref = next(it) if epilogue == "rope" else None
    o_ref = next(it)

    a = a_ref[...]
    acc = jnp.dot(a, w_ref[...], preferred_element_type=F32)
    if dual:
        acc2 = jnp.dot(a, w2_ref[...], preferred_element_type=F32)

    if epilogue == "residual":
        o_ref[...] = (res_ref[...] + acc).astype(o_ref.dtype)
    elif epilogue == "swiglu":
        o_ref[...] = (jax.nn.silu(acc) * acc2).astype(o_ref.dtype)
    elif epilogue == "sglu":
        o_ref[...] = (res_ref[...] + acc * jax.nn.sigmoid(acc2)).astype(o_ref.dtype)
    elif epilogue == "rope":
        q_width, kv_width, kvw, hd = rope_cfg
        col0 = pl.program_id(1) * tn
        in_kv = (col0 >= q_width) & (col0 < q_width + kv_width)
        is_rope = (col0 < q_width) | (in_kv & (((col0 - q_width) // kvw) % 2 == 0))

        @pl.when(is_rope)
        def _():
            cos = cos_ref[...]
            sin = sin_ref[...]
            for c in range(tn // hd):
                x = acc[:, c * hd:(c + 1) * hd]
                o_ref[:, c * hd:(c + 1) * hd] = x * cos + pltpu.roll(x, hd // 2, 1) * sin

        @pl.when(jnp.logical_not(is_rope))
        def _():
            o_ref[...] = acc


def matmul(a, w, *, tm, tn, epilogue, out_dtype=F32, n_out=None, w2_col_off=None, res=None, cos=None,
           sin=None, rope_cfg=None, a_buffers=2, a_time_major=None):
    m, kdim = a.shape
    n = w.shape[1] if n_out is None else n_out
    dual = epilogue in ("swiglu", "sglu")
    assert m % tm == 0
    a_spec = pl.BlockSpec((tm, kdim), lambda i, j: (i, 0))
    if a_buffers != 2:
        a_spec = pl.BlockSpec((tm, kdim), lambda i, j: (i, 0), pipeline_mode=pl.Buffered(a_buffers))
    if a_time_major is not None:
        batch, t_len = a_time_major
        assert t_len % tm == 0 and batch * t_len == m
        nt = t_len // tm
        a = a.reshape(t_len, batch * kdim)
        a_spec = pl.BlockSpec((tm, kdim), lambda i, j: (i % nt, i // nt))
    in_specs = [a_spec, pl.BlockSpec((kdim, tn), lambda i, j: (0, j))]
    args = [a, w]
    if dual:
        assert w2_col_off % tn == 0 and n % tn == 0
        off = w2_col_off // tn
        in_specs.append(pl.BlockSpec((kdim, tn), lambda i, j: (0, j + off)))
        args.append(w)
    if epilogue in ("residual", "sglu"):
        in_specs.append(pl.BlockSpec((tm, tn), lambda i, j: (i, j)))
        args.append(res)
    if epilogue == "rope":
        hd = rope_cfg[3]
        in_specs += [pl.BlockSpec((tm, hd), lambda i, j: (i, 0))] * 2
        args += [cos, sin]
    return pl.pallas_call(
        functools.partial(_mm_kernel, tn=tn, dual=dual, epilogue=epilogue, rope_cfg=rope_cfg),
        grid=(m // tm, pl.cdiv(n, tn)),
        in_specs=in_specs,
        out_specs=pl.BlockSpec((tm, tn), lambda i, j: (i, j)),
        out_shape=jax.ShapeDtypeStruct((m, n), out_dtype),
        compiler_params=_cparams(("parallel", "arbitrary")),
        name="mm_" + epilogue,
    )(*args)


def _compress_kernel(x_ref, pe_ref, w1_ref, w2_ref, o_ref, w1_s, w2_s):
    @pl.when(pl.program_id(0) == 0)
    def _():
        w1_s[...] = w1_ref[...].astype(BF16)
        w2_s[...] = w2_ref[...].astype(BF16)

    x = (x_ref[...] + pe_ref[...]).astype(BF16)
    h = jax.nn.gelu(jnp.dot(x, w1_s[...], preferred_element_type=F32))
    o_ref[...] = jnp.dot(h.astype(BF16), w2_s[...], preferred_element_type=F32).astype(o_ref.dtype)


def compress_rows(flat, pe, w1, w2):
    r, kdim = flat.shape
    hid, hd = w2.shape
    tr = _tile(r, 512, SUBLANE)
    return pl.pallas_call(
        _compress_kernel,
        grid=(r // tr,),
        in_specs=[pl.BlockSpec((tr, kdim), lambda i: (i, 0)),
                  pl.BlockSpec((1, kdim), lambda i: (0, 0)),
                  pl.BlockSpec((kdim, hid), lambda i: (0, 0)),
                  pl.BlockSpec((hid, hd), lambda i: (0, 0))],
        out_specs=pl.BlockSpec((tr, hd), lambda i: (i, 0)),
        out_shape=jax.ShapeDtypeStruct((r, hd), BF16),
        scratch_shapes=[pltpu.VMEM((kdim, hid), BF16), pltpu.VMEM((hid, hd), BF16)],
        compiler_params=_cparams(("arbitrary",)),
        name="compress",
    )(flat, pe.reshape(1, kdim), w1, w2)


def _cmp_flat(x, ncp):
    b, tk, n_kv, hd = x.shape
    n_chunk = tk // CMP_STRIDE
    c = x[:, :n_chunk * CMP_STRIDE].reshape(b, n_chunk, CMP_STRIDE, n_kv, hd)
    blk = jnp.concatenate([c[:, :-1], c[:, 1:]], axis=2)
    flat = blk.transpose(0, 3, 1, 2, 4).reshape(b, n_kv, n_chunk - 1, CMP_LEN * hd)
    return jnp.pad(flat, ((0, 0), (0, 0), (0, ncp - (n_chunk - 1)), (0, 0)))


def _overlap_matrix(ncp, nsp):
    ic = jnp.arange(ncp, dtype=jnp.int32)[:, None] * CMP_STRIDE
    js = jnp.arange(nsp, dtype=jnp.int32)[None, :] * SEL_BLOCK
    return ((ic < js + SEL_BLOCK) & (ic + CMP_LEN > js)).astype(F32)


def _compress_paged_kernel(*refs, pg, page, hd, n_kv, n_groups, n_chunk, ncp):
    pt_ref = refs[0]
    page_refs = refs[1:1 + pg]
    pe_ref, w1_ref, w2_ref, o_ref, la_s, lb_s, ca_s, cb_s = refs[1 + pg:]
    del pt_ref
    g = pl.program_id(1)
    cpp = page // CMP_STRIDE
    rows_g = pg * cpp
    half = CMP_STRIDE * hd

    if ncp > n_chunk:
        @pl.when(g == 0)
        def _():
            ca_s[...] = jnp.zeros_like(ca_s)
            cb_s[...] = jnp.zeros_like(cb_s)

    row0 = pl.multiple_of(g * rows_g, rows_g)
    for j in range(2 * n_kv):
        s = j // n_kv
        for i in range(pg):
            for r in range(CMP_STRIDE):
                x = page_refs[i][pl.ds(r, cpp, stride=CMP_STRIDE), j, :]
                la_s[i * cpp:(i + 1) * cpp, r * hd:(r + 1) * hd] = x + pe_ref[s, r:r + 1, :]
                lb_s[i * cpp:(i + 1) * cpp, r * hd:(r + 1) * hd] = x + pe_ref[s, CMP_STRIDE + r:CMP_STRIDE + r + 1, :]
        ca_s[j, pl.ds(row0, rows_g), :] = jnp.dot(la_s[...].astype(BF16), w1_ref[s, :half, :],
                                                  preferred_element_type=F32)
        cb_s[j, pl.ds(row0, rows_g), :] = jnp.dot(lb_s[...].astype(BF16), w1_ref[s, half:, :],
                                                  preferred_element_type=F32)

    @pl.when(g == n_groups - 1)
    def _():
        for j in range(2 * n_kv):
            s, k = divmod(j, n_kv)
            pre = ca_s[j] + pltpu.roll(cb_s[j], ncp - 1, 0)
            h = jax.nn.gelu(pre)
            o_ref[s, k] = jnp.dot(h.astype(BF16), w2_ref[s].astype(BF16),
                                  preferred_element_type=F32).astype(o_ref.dtype)


def compress_paged(page_table, cache4, pe, w1, w2, *, n_kv, hd, ncp):
    bd, n_pages = page_table.shape
    page = cache4.shape[1]
    assert page % CMP_STRIDE == 0
    cpp = page // CMP_STRIDE
    n_chunk = n_pages * cpp
    assert ncp >= n_chunk and cpp == SUBLANE and 2 * n_kv == SUBLANE
    pg = _tile(n_pages, 16, 1)
    n_groups = n_pages // pg
    hid = w1.shape[-1]

    def page_spec(i):
        return pl.BlockSpec((None, page, 2 * n_kv, hd), lambda b, g, pt: (pt[b * n_pages + g * pg + i], 0, 0, 0))

    kern = functools.partial(_compress_paged_kernel, pg=pg, page=page, hd=hd, n_kv=n_kv, n_groups=n_groups,
                             n_chunk=n_chunk, ncp=ncp)
    grid_spec = pltpu.PrefetchScalarGridSpec(
        num_scalar_prefetch=1,
        grid=(bd, n_groups),
        in_specs=[page_spec(i) for i in range(pg)] + [
            pl.BlockSpec((2, CMP_LEN, hd), lambda b, g, pt: (0, 0, 0)),
            pl.BlockSpec((2, CMP_LEN * hd, hid), lambda b, g, pt: (0, 0, 0)),
            pl.BlockSpec((2, hid, hd), lambda b, g, pt: (0, 0, 0))],
        out_specs=pl.BlockSpec((2, None, n_kv, ncp, hd), lambda b, g, pt: (0, b, 0, 0, 0)),
        scratch_shapes=[pltpu.VMEM((pg * cpp, CMP_STRIDE * hd), F32), pltpu.VMEM((pg * cpp, CMP_STRIDE * hd), F32),
                        pltpu.VMEM((2 * n_kv, ncp, hid), F32), pltpu.VMEM((2 * n_kv, ncp, hid), F32)],
    )
    return pl.pallas_call(
        kern,
        grid_spec=grid_spec,
        out_shape=jax.ShapeDtypeStruct((2, bd, n_kv, ncp, hd), BF16),
        compiler_params=_cparams(("arbitrary", "arbitrary")),
        name="compress_paged",
    )(page_table.reshape(-1), *([cache4] * pg), pe, w1, w2)


def _nsa_prompt_kernel(q_ref, ks_ref, vs_ref, kw_ref, vw_ref, kc_ref, vc_ref, gate_ref, ovl_ref, exp_ref,
                       o_ref, ks_s, vs_s, kw_s, vw_s, osel_s, *, gqa, hd, t_len, n_cmp, n_sel, k_sel, sel_ext,
                       scale):
    qb = pl.program_id(2)

    @pl.when(qb == 0)
    def _():
        ks_s[...] = ks_ref[...].astype(BF16)
        vs_s[...] = vs_ref[...].astype(BF16)
        kw_s[...] = kw_ref[...].astype(BF16)
        vw_s[...] = vw_ref[...].astype(BF16)

    rows = gqa * Q_BLK
    b0 = qb * Q_BLK
    q = q_ref[...]
    qg = jnp.concatenate([q[:, g * hd:(g + 1) * hd] for g in range(gqa)], axis=0).astype(BF16)
    pos_q = b0 + lax.broadcasted_iota(jnp.int32, (Q_BLK, 1), 0)

    c_exp = scale * LOG2_E

    def exp_shifted(s3):
        return jnp.exp2((s3 - jnp.max(s3, axis=-1, keepdims=True)) * c_exp)

    ncp = kc_ref.shape[0]
    s = lax.dot_general(qg, kc_ref[...], _NT, preferred_element_type=F32)
    ci = lax.broadcasted_iota(jnp.int32, (1, ncp), 1)
    m_cmp = ((ci * CMP_STRIDE + (CMP_LEN - 1)) <= pos_q) & (ci < n_cmp)
    s3 = s.reshape(gqa, Q_BLK, ncp) + jnp.where(m_cmp, 0.0, NEG_INF)[None]
    e3 = exp_shifted(s3) * m_cmp.astype(F32)[None]
    l3 = jnp.sum(e3, axis=-1, keepdims=True)
    p3 = e3 * jnp.where(l3 > 0.0, 1.0 / l3, 0.0)
    o_cmp = jnp.dot(p3.reshape(rows, ncp).astype(BF16), vc_ref[...], preferred_element_type=F32)

    nsp = ovl_ref.shape[1]
    imp = jnp.dot(jnp.sum(p3, axis=0), ovl_ref[...], preferred_element_type=F32,
                  precision=lax.Precision.HIGHEST)
    jj = lax.broadcasted_iota(jnp.int32, (1, nsp), 1)
    cur = pos_q // SEL_BLOCK
    forced = (jj == 0) | (jj == cur) | (jj == cur - 1)
    future = jj * SEL_BLOCK > pos_q
    score = jnp.where(future, -jnp.inf, jnp.where(forced, jnp.inf, imp))
    rank = jnp.zeros((Q_BLK, nsp), F32)
    for j2 in range(n_sel):
        col = score[:, j2:j2 + 1]
        beats = (col > score) | ((col == score) & (j2 < jj))
        rank = rank + beats.astype(F32)
    sel = ((rank < k_sel) & (jj < n_sel)).astype(BF16)
    selk = jnp.dot(sel, exp_ref[...], preferred_element_type=F32)

    kpos = lax.broadcasted_iota(jnp.int32, (1, t_len), 1)
    bias_sel = jnp.where((selk > 0.5) & (kpos <= pos_q), 0.0, NEG_INF)

    def sel_branch(kv_len):
        s = lax.dot_general(qg, ks_s[:kv_len, :], _NT, preferred_element_type=F32)
        e3 = exp_shifted(s.reshape(gqa, Q_BLK, kv_len) + bias_sel[:, :kv_len][None])
        l = jnp.sum(e3, axis=-1, keepdims=True).reshape(rows, 1)
        osel_s[...] = jnp.dot(e3.reshape(rows, kv_len).astype(BF16), vs_s[:kv_len, :],
                              preferred_element_type=F32) / l

    n_ext = (b0 + Q_BLK + sel_ext - 1) // sel_ext
    for c in range(1, t_len // sel_ext + 1):
        pl.when(n_ext == c)(functools.partial(sel_branch, c * sel_ext))
    o_sel = osel_s[...]

    wk = WINDOW + Q_BLK
    start = pl.multiple_of(jnp.maximum(b0 - WINDOW, 0), Q_BLK)
    s = lax.dot_general(qg, kw_s[pl.ds(start, wk), :], _NT, preferred_element_type=F32)
    dist = pos_q - (start + lax.broadcasted_iota(jnp.int32, (1, wk), 1))
    m_win = (dist >= 0) & (dist <= WINDOW)
    s3 = s.reshape(gqa, Q_BLK, wk) + jnp.where(m_win, 0.0, NEG_INF)[None]
    e3 = exp_shifted(s3)
    l_win = jnp.sum(e3, axis=-1, keepdims=True).reshape(rows, 1)
    o_win = jnp.dot(e3.reshape(rows, wk).astype(BF16), vw_s[pl.ds(start, wk), :],
                    preferred_element_type=F32) / l_win

    gate = jax.nn.sigmoid(gate_ref[...])
    for g in range(gqa):
        r = slice(g * Q_BLK, (g + 1) * Q_BLK)
        o = (gate[:, 3 * g:3 * g + 1] * o_cmp[r] + gate[:, 3 * g + 1:3 * g + 2] * o_sel[r]
             + gate[:, 3 * g + 2:3 * g + 3] * o_win[r])
        o_ref[:, g * hd:(g + 1) * hd] = o.astype(o_ref.dtype)


def nsa_prompt_attention(proj, gates_r, kc, vc, *, batch, t_len, n_kv, gqa, hd, n_cmp):
    m = batch * t_len
    nq = t_len // Q_BLK
    q_width = n_kv * gqa * hd
    kvw = n_kv * hd
    ncp = kc.shape[2]
    n_sel = -(-t_len // SEL_BLOCK)
    nsp = -(-n_sel // LANE) * LANE
    k_sel = min(N_SELECT, n_sel)
    assert t_len % Q_BLK == 0 and t_len >= WINDOW + Q_BLK and t_len % SEL_BLOCK == 0
    ovl = _overlap_matrix(ncp, nsp)
    expand = (jnp.arange(nsp, dtype=jnp.int32)[:, None]
              == jnp.arange(t_len, dtype=jnp.int32)[None, :] // SEL_BLOCK).astype(BF16)

    def kv_spec(slot):
        base = (q_width + slot * kvw) // hd
        return pl.BlockSpec((t_len, hd), lambda b, k, i: (b, base + k))

    sel_ext = _tile(t_len, max(t_len // 4, LANE))
    kern = functools.partial(_nsa_prompt_kernel, gqa=gqa, hd=hd, t_len=t_len, n_cmp=n_cmp, n_sel=n_sel,
                             k_sel=k_sel, sel_ext=sel_ext, scale=hd ** -0.5)
    return pl.pallas_call(
        kern,
        grid=(batch, n_kv, nq),
        in_specs=[pl.BlockSpec((Q_BLK, gqa * hd), lambda b, k, i: (b * nq + i, k)),
                  kv_spec(2), kv_spec(3), kv_spec(4), kv_spec(5),
                  pl.BlockSpec((None, None, ncp, hd), lambda b, k, i: (b, k, 0, 0)),
                  pl.BlockSpec((None, None, ncp, hd), lambda b, k, i: (b, k, 0, 0)),
                  pl.BlockSpec((None, Q_BLK, 3 * gqa), lambda b, k, i: (k, b * nq + i, 0)),
                  pl.BlockSpec((ncp, nsp), lambda b, k, i: (0, 0)),
                  pl.BlockSpec((nsp, t_len), lambda b, k, i: (0, 0))],
        out_specs=pl.BlockSpec((Q_BLK, gqa * hd), lambda b, k, i: (b * nq + i, k)),
        out_shape=jax.ShapeDtypeStruct((m, q_width), BF16),
        scratch_shapes=[pltpu.VMEM((t_len, hd), BF16)] * 4 + [pltpu.VMEM((gqa * Q_BLK, hd), F32)],
        compiler_params=_cparams(("parallel", "parallel", "arbitrary")),
        name="nsa_prompt",
    )(proj, proj, proj, proj, proj, kc, vc, gates_r, ovl, expand)


def _nsa_dec_select_kernel(q_ref, kc_ref, vc_ref, ovl_ref, ocmp_ref, idx_ref, *, n_cmp, n_sel, pos_q, scale):
    q = q_ref[...].astype(BF16)
    ncp = kc_ref.shape[0]
    nsp = ovl_ref.shape[1]
    s = lax.dot_general(q, kc_ref[...], _NT, preferred_element_type=F32) * scale
    ci = lax.broadcasted_iota(jnp.int32, (1, ncp), 1)
    m_cmp = ((ci * CMP_STRIDE + (CMP_LEN - 1)) <= pos_q) & (ci < n_cmp)
    s = jnp.where(m_cmp, s, NEG_INF)
    e = jnp.where(m_cmp, jnp.exp(s - jnp.max(s, axis=-1, keepdims=True)), 0.0)
    l = jnp.sum(e, axis=-1, keepdims=True)
    p = e * jnp.where(l > 0.0, 1.0 / l, 0.0)
    ocmp_ref[...] = jnp.dot(p.astype(BF16), vc_ref[...], preferred_element_type=F32)

    imp = jnp.sum(jnp.dot(p, ovl_ref[...], preferred_element_type=F32, precision=lax.Precision.HIGHEST),
                  axis=0, keepdims=True)
    jj = lax.broadcasted_iota(jnp.int32, (1, nsp), 1)
    cur = pos_q // SEL_BLOCK
    forced = (jj == 0) | (jj == cur) | (jj == cur - 1)
    future = jj * SEL_BLOCK > pos_q
    score = jnp.where(future, -jnp.inf, jnp.where(forced, jnp.inf, imp))
    rr = lax.broadcasted_iota(jnp.int32, (nsp, nsp), 0)
    cc = lax.broadcasted_iota(jnp.int32, (nsp, nsp), 1)
    score_b = jnp.broadcast_to(score, (nsp, nsp))
    score_col = jnp.sum(jnp.where(rr == cc, score_b, 0.0), axis=1, keepdims=True)
    beaten = ((score_b > score_col) | ((score_b == score_col) & (cc < rr))) & (cc < n_sel)
    rank_col = jnp.sum(beaten.astype(F32), axis=1, keepdims=True)
    lane = lax.broadcasted_iota(jnp.int32, (nsp, LANE), 1)
    row = lax.broadcasted_iota(jnp.int32, (nsp, LANE), 0)
    hit = (rank_col == lane.astype(F32)) & (row < n_sel)
    idx_ref[...] = jnp.sum(jnp.where(hit, row.astype(F32), 0.0), axis=0, keepdims=True).astype(jnp.int32)


def nsa_dec_select(q, kc, vc, *, n_cmp, n_sel, pos_q):
    bd, n_kv, gqa, hd = q.shape
    ncp = kc.shape[2]
    nsp = -(-n_sel // LANE) * LANE
    kern = functools.partial(_nsa_dec_select_kernel, n_cmp=n_cmp, n_sel=n_sel, pos_q=pos_q, scale=hd ** -0.5)
    return pl.pallas_call(
        kern,
        grid=(bd, n_kv),
        in_specs=[pl.BlockSpec((None, None, gqa, hd), lambda b, k: (b, k, 0, 0)),
                  pl.BlockSpec((None, None, ncp, hd), lambda b, k: (b, k, 0, 0)),
                  pl.BlockSpec((None, None, ncp, hd), lambda b, k: (b, k, 0, 0)),
                  pl.BlockSpec((ncp, nsp), lambda b, k: (0, 0))],
        out_specs=[pl.BlockSpec((None, None, gqa, hd), lambda b, k: (b, k, 0, 0)),
                   pl.BlockSpec((None, None, 1, LANE), lambda b, k: (b, k, 0, 0))],
        out_shape=[jax.ShapeDtypeStruct((bd, n_kv, gqa, hd), F32),
                   jax.ShapeDtypeStruct((bd, n_kv, 1, LANE), jnp.int32)],
        compiler_params=_cparams(("parallel", "parallel")),
        name="nsa_dec_select",
    )(q, kc, vc, _overlap_matrix(ncp, nsp))


def _nsa_dec_attend_kernel(*refs, k_sel, n_kv, past_len, pos_q, scale):
    pt_ref, idx_ref, q_ref = refs[:3]
    kv_refs = refs[3:3 + k_sel]
    new_ref, kw_ref, vw_ref, ocmp_ref, gate_ref, o_ref = refs[3 + k_sel:]
    del pt_ref
    b = pl.program_id(0)
    k = pl.program_id(1)

    q = q_ref[...].astype(BF16)
    new = new_ref[...].astype(BF16)
    newf = new.astype(F32)
    s_all = lax.dot_general(q, new, _NT, preferred_element_type=F32) * scale

    first = (lax.broadcasted_iota(jnp.int32, (SEL_BLOCK, 1), 0) == 0).astype(F32)
    new_k = first * new_ref[2:3, :]
    new_v = first * new_ref[3:4, :]
    off = lax.broadcasted_iota(jnp.int32, (1, SEL_BLOCK), 1)
    s_blk, v_blk = [], []
    for r in range(k_sel):
        tok0 = idx_ref[(b * n_kv + k) * k_sel + r] * SEL_BLOCK
        is_new = tok0 >= past_len
        kb = jnp.where(is_new, new_k, kv_refs[r][:, k, :]).astype(BF16)
        v_blk.append(jnp.where(is_new, new_v, kv_refs[r][:, n_kv + k, :]).astype(BF16))
        s = lax.dot_general(q, kb, _NT, preferred_element_type=F32) * scale
        s_blk.append(jnp.where(tok0 + off <= pos_q, s, NEG_INF))
    mx = functools.reduce(jnp.maximum, [jnp.max(s, axis=-1, keepdims=True) for s in s_blk])
    l_sel = jnp.zeros_like(mx)
    acc = jnp.zeros((q.shape[0], q.shape[1]), F32)
    for s, vb in zip(s_blk, v_blk):
        e = jnp.exp(s - mx)
        l_sel = l_sel + jnp.sum(e, axis=-1, keepdims=True)
        acc = acc + jnp.dot(e.astype(BF16), vb, preferred_element_type=F32)
    o_sel = acc / l_sel

    wb = kw_ref.shape[0]
    s = lax.dot_general(q, kw_ref[...].astype(BF16), _NT, preferred_element_type=F32) * scale
    pos_w = (past_len - wb) + lax.broadcasted_iota(jnp.int32, (1, wb), 1)
    dist = pos_q - pos_w
    m_win = (dist >= 0) & (dist <= WINDOW) & (pos_w >= 0)
    s = jnp.where(m_win, s, NEG_INF)
    s_nw = s_all[:, 4:5]
    mx = jnp.maximum(jnp.max(s, axis=-1, keepdims=True), s_nw)
    e = jnp.where(m_win, jnp.exp(s - mx), 0.0)
    e_nw = jnp.exp(s_nw - mx)
    l_w = jnp.sum(e, axis=-1, keepdims=True) + e_nw
    o_win = (jnp.dot(e.astype(BF16), vw_ref[...].astype(BF16), preferred_element_type=F32)
             + e_nw.astype(BF16).astype(F32) * newf[5:6, :]) / l_w

    gate = jax.nn.sigmoid(gate_ref[...])
    o = gate[:, 0:1] * ocmp_ref[...] + gate[:, 1:2] * o_sel + gate[:, 2:3] * o_win
    o_ref[...] = o.astype(o_ref.dtype)


def nsa_dec_attend(page_table, idx, q, cache4, new_rows, win2d, o_cmp, gates, *, k_sel, past_len, pos_q):
    bd, n_kv, gqa, hd = q.shape
    n_pages = page_table.shape[1]
    page = cache4.shape[1]
    wb = win2d.shape[1]
    assert page % SEL_BLOCK == 0 and past_len % SEL_BLOCK == 0 and pos_q == past_len
    assert 2 * n_kv == SUBLANE and cache4.shape[2] == 4 * n_kv
    bpp = page // SEL_BLOCK
    kern = functools.partial(_nsa_dec_attend_kernel, k_sel=k_sel, n_kv=n_kv, past_len=past_len, pos_q=pos_q,
                             scale=hd ** -0.5)

    def blk_spec(r):
        def index_map(b, k, pt, ix):
            j = ix[(b * n_kv + k) * k_sel + r]
            return pt[b * n_pages + jnp.minimum(j // bpp, n_pages - 1)], j % bpp, 1, 0

        return pl.BlockSpec((None, SEL_BLOCK, 2 * n_kv, hd), index_map)

    grid_spec = pltpu.PrefetchScalarGridSpec(
        num_scalar_prefetch=2,
        grid=(bd, n_kv),
        in_specs=[pl.BlockSpec((None, None, gqa, hd), lambda b, k, pt, ix: (b, k, 0, 0))]
        + [blk_spec(r) for r in range(k_sel)]
        + [pl.BlockSpec((None, None, SUBLANE, hd), lambda b, k, pt, ix: (b, k, 0, 0)),
           pl.BlockSpec((None, wb, hd), lambda b, k, pt, ix: (b, 0, k)),
           pl.BlockSpec((None, wb, hd), lambda b, k, pt, ix: (b, 0, n_kv + k)),
           pl.BlockSpec((None, None, gqa, hd), lambda b, k, pt, ix: (b, k, 0, 0)),
           pl.BlockSpec((None, None, gqa, 3), lambda b, k, pt, ix: (b, k, 0, 0))],
        out_specs=pl.BlockSpec((None, None, gqa, hd), lambda b, k, pt, ix: (b, k, 0, 0)),
    )
    return pl.pallas_call(
        kern,
        grid_spec=grid_spec,
        out_shape=jax.ShapeDtypeStruct((bd, n_kv, gqa, hd), BF16),
        compiler_params=_cparams(("arbitrary", "arbitrary")),
        name="nsa_dec_attend",
    )(page_table.reshape(-1), idx, q, *([cache4] * k_sel), new_rows, win2d, win2d, o_cmp, gates)


N_POW = SUBLANE


def _s5_disc_kernel(lre_ref, lim_ref, ldt_ref, bre_ref, bim_ref, bbre_ref, bbim_ref, pre_ref, pim_ref):
    lam_re = lre_ref[...]
    lam_im = lim_ref[...]
    dt = jnp.exp(ldt_ref[...])
    mag = jnp.exp(lam_re * dt)
    ab_re = mag * jnp.cos(lam_im * dt)
    ab_im = mag * jnp.sin(lam_im * dt)
    nr, ni = ab_re - 1.0, ab_im
    den = lam_re * lam_re + lam_im * lam_im
    f_re = (nr * lam_re + ni * lam_im) / den
    f_im = (ni * lam_re - nr * lam_im) / den
    b_re = bre_ref[...]
    b_im = bim_ref[...]
    bbre_ref[...] = f_re * b_re - f_im * b_im
    bbim_ref[...] = f_re * b_im + f_im * b_re
    pr, pi = ab_re, ab_im
    pre_ref[0] = pr
    pim_ref[0] = pi
    for n in range(1, N_POW):
        pr, pi = pr * ab_re - pi * ab_im, pr * ab_im + pi * ab_re
        pre_ref[n] = pr
        pim_ref[n] = pi


def s5_discretize(lam_re, lam_im, log_dt, b_re, b_im):
    g, n, c = b_re.shape
    rep = lambda x: jnp.broadcast_to(x[:, None, :], (g, c, n)).reshape(g * c, n)
    ldt = jnp.broadcast_to(log_dt[:, None, None], (g, c, n)).reshape(g * c, n)
    bt = lambda x: x.transpose(0, 2, 1).reshape(g * c, n)
    full = pl.BlockSpec((g * c, n), lambda: (0, 0))
    pw = pl.BlockSpec((N_POW, g * c, n), lambda: (0, 0, 0))
    bbre, bbim, pre, pim = pl.pallas_call(
        _s5_disc_kernel,
        in_specs=[full] * 5,
        out_specs=[full, full, pw, pw],
        out_shape=[jax.ShapeDtypeStruct((g * c, n), F32)] * 2
        + [jax.ShapeDtypeStruct((N_POW, g * c, n), F32)] * 2,
        compiler_params=pltpu.CompilerParams(vmem_limit_bytes=VMEM_LIMIT_BYTES),
        name="s5_discretize",
    )(rep(lam_re), rep(lam_im), ldt, bt(b_re), bt(b_im))
    pick = lambda x: x.reshape(N_POW, g, c, n)[:, :, 0, :]
    return bbre.reshape(g, c, n), bbim.reshape(g, c, n), pick(pre), pick(pim)


def _s5_scan_kernel(u_ref, h0r_ref, h0i_ref, bbr_ref, bbi_ref, ccr_ref, cci_ref, d_ref, mr_ref, mi_ref,
                    pr_ref, pi_ref, z_ref, hr_ref, hi_ref, xr_s, xi_s, cr_s, ci_s, *, batch, n_groups, n_t):
    t = pl.program_id(1)
    shifts = [batch << n for n in range(8) if (batch << n) < SUBLANE]
    row = lax.broadcasted_iota(jnp.int32, (SUBLANE, 1), 0)

    @pl.when(t == 0)
    def _():
        cr_s[...] = h0r_ref[...]
        ci_s[...] = h0i_ref[...]

    u = u_ref[...]
    ub = u.astype(BF16)
    xr_s[...] = jnp.dot(ub, bbr_ref[...].astype(BF16), preferred_element_type=F32)
    xi_s[...] = jnp.dot(ub, bbi_ref[...].astype(BF16), preferred_element_type=F32)
    pr = pr_ref[...]
    pi = pi_ref[...]

    def group(r, carry):
        cr, ci = carry
        rows = pl.ds(pl.multiple_of(r * SUBLANE, SUBLANE), SUBLANE)
        xr = xr_s[rows, :]
        xi = xi_s[rows, :]
        for n, sh in enumerate(shifts):
            mr = mr_ref[n]
            mi = mi_ref[n]
            sr = pltpu.roll(xr, sh, 0)
            si = pltpu.roll(xi, sh, 0)
            xr, xi = xr + (mr * sr - mi * si), xi + (mr * si + mi * sr)
        xr, xi = xr + (pr * cr - pi * ci), xi + (pr * ci + pi * cr)
        xr_s[rows, :] = xr
        xi_s[rows, :] = xi
        cr, ci = xr, xi
        for sh in shifts:
            keep = row >= SUBLANE - sh
            cr = jnp.where(keep, cr, pltpu.roll(cr, SUBLANE - sh, 0))
            ci = jnp.where(keep, ci, pltpu.roll(ci, SUBLANE - sh, 0))
        return cr, ci

    cr, ci = lax.fori_loop(0, n_groups, group, (cr_s[...], ci_s[...]))
    cr_s[...] = cr
    ci_s[...] = ci

    y = (jnp.dot(xr_s[...].astype(BF16), ccr_ref[...].astype(BF16), preferred_element_type=F32)
         - jnp.dot(xi_s[...].astype(BF16), cci_ref[...].astype(BF16), preferred_element_type=F32)
         + d_ref[...] * u)
    z_ref[...] = jax.nn.gelu(y).astype(z_ref.dtype)

    @pl.when(t == n_t - 1)
    def _():
        hr_ref[...] = cr
        hi_ref[...] = ci


def s5_scan(u_tb, h0_re, h0_im, prm, *, batch, t_len):
    bb_re, bb_im, cc_re, cc_im, d_skip, pw_re, pw_im = prm
    n_slab, cw, sw = bb_re.shape
    assert SUBLANE % batch == 0
    tpg = SUBLANE // batch
    assert t_len % tpg == 0
    tt = _tile(t_len, 128, tpg)
    n_t = t_len // tt
    n_groups = tt // tpg
    shifts = [batch << n for n in range(8) if (batch << n) < SUBLANE]
    row = jnp.arange(SUBLANE)[None, :, None]

    def shift_tab(pw):
        tabs = [jnp.where(row >= sh, pw[sh // batch - 1][:, None, :], 0.0) for sh in shifts]
        return jnp.stack(tabs, axis=1) if tabs else jnp.zeros((n_slab, 1, SUBLANE, sw), F32)

    carry_tab = lambda pw: pw[jnp.arange(SUBLANE) // batch].transpose(1, 0, 2)
    tile_rows = lambda h: jnp.tile(h.reshape(batch, n_slab, sw).transpose(1, 0, 2), (1, tpg, 1))
    n_tab = max(len(shifts), 1)

    slab3 = lambda r, c: pl.BlockSpec((None, r, c), lambda s, t: (s, 0, 0))
    tab4 = pl.BlockSpec((None, n_tab, SUBLANE, sw), lambda s, t: (s, 0, 0, 0))
    kern = functools.partial(_s5_scan_kernel, batch=batch, n_groups=n_groups, n_t=n_t)
    z, hr, hi = pl.pallas_call(
        kern,
        grid=(n_slab, n_t),
        in_specs=[pl.BlockSpec((tt * batch, cw), lambda s, t: (t, s)),
                  slab3(SUBLANE, sw), slab3(SUBLANE, sw),
                  slab3(cw, sw), slab3(cw, sw), slab3(sw, cw), slab3(sw, cw),
                  pl.BlockSpec((1, cw), lambda s, t: (0, s)),
                  tab4, tab4,
                  slab3(SUBLANE, sw), slab3(SUBLANE, sw)],
        out_specs=[pl.BlockSpec((tt * batch, cw), lambda s, t: (t, s)), slab3(SUBLANE, sw), slab3(SUBLANE, sw)],
        out_shape=[jax.ShapeDtypeStruct((t_len * batch, n_slab * cw), BF16),
                   jax.ShapeDtypeStruct((n_slab, SUBLANE, sw), F32),
                   jax.ShapeDtypeStruct((n_slab, SUBLANE, sw), F32)],
        scratch_shapes=[pltpu.VMEM((tt * batch, sw), F32), pltpu.VMEM((tt * batch, sw), F32),
                        pltpu.VMEM((SUBLANE, sw), F32), pltpu.VMEM((SUBLANE, sw), F32)],
        compiler_params=_cparams(("parallel", "arbitrary")),
        name="s5_scan",
    )(u_tb, tile_rows(h0_re), tile_rows(h0_im), bb_re, bb_im, cc_re, cc_im, d_skip.reshape(1, -1),
      shift_tab(pw_re), shift_tab(pw_im), carry_tab(pw_re), carry_tab(pw_im))
    final = lambda h: h[:, :batch].transpose(1, 0, 2).reshape(h0_re.shape)
    return z, final(hr), final(hi)


def s5_prepare(lam_re, lam_im, log_dt, b_re, b_im, c_re, c_im, d_skip):
    g, n, c = b_re.shape
    gps = LANE // c
    n_slab = g // gps
    sw = gps * n
    bbt_re, bbt_im, pw_re, pw_im = s5_discretize(lam_re, lam_im, log_dt, b_re, b_im)
    eye = jnp.eye(gps, dtype=F32)

    def blockdiag_in(bt):
        x = bt.reshape(n_slab, gps, c, 1, n) * eye[None, :, None, :, None]
        return x.reshape(n_slab, gps * c, sw)

    def blockdiag_out(cm):
        x = cm.transpose(0, 2, 1).reshape(n_slab, gps, n, 1, c) * eye[None, :, None, :, None]
        return x.reshape(n_slab, sw, gps * c)

    return (blockdiag_in(bbt_re), blockdiag_in(bbt_im), blockdiag_out(c_re.astype(F32)),
            blockdiag_out(c_im.astype(F32)), d_skip.astype(F32), pw_re.reshape(N_POW, n_slab, sw),
            pw_im.reshape(N_POW, n_slab, sw))


MM_ROWS = 1024


def _rope_tables(pos, hd):
    half = hd // 2
    inv = ROPE_THETA ** (-jnp.arange(half, dtype=F32) / half)
    ang = pos.astype(F32)[:, None] * inv[None, :]
    cos = jnp.cos(ang)
    sin = jnp.sin(ang)
    return jnp.concatenate([cos, cos], axis=-1), jnp.concatenate([-sin, sin], axis=-1)


def in_proj(u, w_in, pos, *, q_width, kv_width, kvw, hd):
    m = u.shape[0]
    cos, sin = _rope_tables(pos, hd)
    return matmul(u, w_in, tm=_tile(m, MM_ROWS, SUBLANE), tn=kvw, epilogue="rope", cos=cos, sin=sin,
                  rope_cfg=(q_width, kv_width, kvw, hd))


def out_proj(o, w, res):
    m, d = res.shape
    return matmul(o, w, tm=_tile(m, MM_ROWS, SUBLANE), tn=_tile(d, 512), epilogue="residual", res=res)


def glu_proj(z, w_glu, res, time_major=None):
    m, d = res.shape
    rows = m if time_major is None else time_major[1]
    return matmul(z, w_glu, tm=_tile(rows, MM_ROWS, SUBLANE), tn=_tile(d, 512), epilogue="sglu", n_out=d,
                  w2_col_off=d, res=res, a_time_major=time_major)


def ffn(h, hn, w_gu, w_down):
    m, d = h.shape
    d_ff = w_down.shape[0]
    tm = _tile(m, MM_ROWS, SUBLANE)
    act = matmul(hn, w_gu, tm=tm, tn=_tile(d_ff, 512), epilogue="swiglu", out_dtype=BF16, n_out=d_ff,
                 w2_col_off=d_ff)
    return matmul(act, w_down, tm=tm, tn=_tile(d, 256), epilogue="residual", res=h, a_buffers=1)


def kernel(x_prompt, x_sample, cache_nsa, cache_win, state_s5_re, state_s5_im, page_table, norm_mix, norm_ffn,
           norm_final, att_w_in, att_w_out, cmp_pe, cmp_w1, cmp_w2, s5_lambda_re, s5_lambda_im, s5_log_dt,
           s5_b_re, s5_b_im, s5_c_re, s5_c_im, s5_d, s5_w_glu, ffn_w_gate_up, ffn_w_down):
    batch, t_len, d = x_prompt.shape
    bd, ts, _ = x_sample.shape
    _, n_pool, page, _, n_kv, hd = cache_nsa.shape
    wb = cache_win.shape[2]
    n_pages = page_table.shape[1]
    past_len = n_pages * page
    depth = norm_mix.shape[0]
    n_heads = d // hd
    gqa = n_heads // n_kv
    q_width = n_heads * hd
    kvw = n_kv * hd
    kv_width = 6 * kvw
    assert ts == 1 and hd == LANE and wb == min(WINDOW, past_len)
    mp = batch * t_len

    att_w_in_b, att_w_out_b = cast_bf16(att_w_in), cast_bf16(att_w_out)
    s5_w_glu_b = cast_bf16(s5_w_glu)
    cmp_w1_b = cast_bf16(cmp_w1)
    ffn_w_gu_b, ffn_w_down_b = cast_bf16(ffn_w_gate_up), cast_bf16(ffn_w_down)

    hp = x_prompt.reshape(mp, d)
    hs = x_sample.reshape(bd, d)
    kv_p, kv_s, win_p, win_s = [], [], [], []
    sr_p, si_p, sr_s, si_s = [], [], [], []
    pos_p = jnp.tile(jnp.arange(t_len, dtype=jnp.int32), batch)
    pos_s = jnp.full((bd,), past_len, jnp.int32)

    for i in range(depth):
        li = i // 2
        if i % 2 == 0:
            up = rmsnorm(hp, norm_mix[i], BF16)
            us = rmsnorm(hs, norm_mix[i], BF16)
            w_in, w_out, pe, w1, w2 = att_w_in_b[li], att_w_out_b[li], cmp_pe[li], cmp_w1[li], cmp_w2[li]
            cfg = dict(q_width=q_width, kv_width=kv_width, kvw=kvw, hd=hd)

            proj = in_proj(up, w_in, pos_p, **cfg)
            proj3 = proj.reshape(batch, t_len, -1)
            kv_p.append(proj3[:, :, q_width:q_width + 4 * kvw].reshape(batch, t_len, 4, n_kv, hd))
            w_keep = min(WINDOW, t_len)
            win_p.append(proj3[:, t_len - w_keep:, q_width + 4 * kvw:q_width + 6 * kvw]
                         .reshape(batch, w_keep, 2, n_kv, hd))
            n_cmp = t_len // CMP_STRIDE - 1
            ncp = -(-n_cmp // LANE) * LANE
            kc, vc = [
                compress_rows(
                    _cmp_flat(proj3[:, :, q_width + s * kvw:q_width + (s + 1) * kvw]
                              .reshape(batch, t_len, n_kv, hd), ncp).reshape(batch * n_kv * ncp, CMP_LEN * hd),
                    pe[s], w1[s], w2[s]).reshape(batch, n_kv, ncp, hd)
                for s in (0, 1)]
            gates_r = proj[:, q_width + kv_width:].reshape(mp, n_kv, 3 * gqa).transpose(1, 0, 2)
            o = nsa_prompt_attention(proj, gates_r, kc, vc, batch=batch, t_len=t_len, n_kv=n_kv, gqa=gqa,
                                     hd=hd, n_cmp=n_cmp)
            hp = out_proj(o, w_out, hp)

            proj_s = in_proj(us, w_in, pos_s, **cfg)
            rows_s = proj_s[:, q_width:q_width + kv_width].reshape(bd, 6, n_kv, hd)
            kv_s.append(rows_s[:, None, :4])
            win_s.append(jnp.concatenate([cache_win[li][:, ts:], rows_s[:, None, 4:]], axis=1))
            tk_len = past_len + ts
            assert past_len % CMP_STRIDE == 0 and ts < CMP_STRIDE
            n_cmp_s = tk_len // CMP_STRIDE - 1
            ncp_s = -(-n_cmp_s // LANE) * LANE
            cache4 = cache_nsa[li].reshape(n_pool, page, 4 * n_kv, hd)
            kvc = compress_paged(page_table, cache4, pe, cmp_w1_b[li], w2, n_kv=n_kv, hd=hd, ncp=ncp_s)
            q_s = proj_s[:, :q_width].reshape(bd, n_kv, gqa, hd)
            n_sel_s = -(-tk_len // SEL_BLOCK)
            k_sel_s = min(N_SELECT, n_sel_s)
            o_cmp, idx = nsa_dec_select(q_s, kvc[0], kvc[1], n_cmp=n_cmp_s, n_sel=n_sel_s, pos_q=past_len)
            new_rows = jnp.pad(rows_s.transpose(0, 2, 1, 3), ((0, 0), (0, 0), (0, SUBLANE - 6), (0, 0)))
            gates_s = proj_s[:, q_width + kv_width:].reshape(bd, n_kv, gqa, 3)
            o_s = nsa_dec_attend(page_table, idx[:, :, 0, :k_sel_s].reshape(-1), q_s, cache4, new_rows,
                                 cache_win[li].reshape(bd, wb, 2 * kvw), o_cmp, gates_s, k_sel=k_sel_s,
                                 past_len=past_len, pos_q=past_len)
            hs = out_proj(o_s.reshape(bd, q_width), w_out, hs)
        else:
            up = rmsnorm(hp, norm_mix[i], F32, time_major=(batch, t_len))
            us = rmsnorm(hs, norm_mix[i], F32)
            prm = s5_prepare(s5_lambda_re[li], s5_lambda_im[li], s5_log_dt[li], s5_b_re[li], s5_b_im[li],
                             s5_c_re[li], s5_c_im[li], s5_d[li])
            g_cnt, n_state = s5_lambda_re.shape[1:]
            w_glu = s5_w_glu_b[li]

            zeros = jnp.zeros((batch, g_cnt, n_state), F32)
            z, hr, hi = s5_scan(up, zeros, zeros, prm, batch=batch, t_len=t_len)
            sr_p.append(hr)
            si_p.append(hi)
            hp = glu_proj(z, w_glu, hp, time_major=(batch, t_len))

            z, hr, hi = s5_scan(us, state_s5_re[li].astype(F32), state_s5_im[li].astype(F32), prm,
                                batch=bd, t_len=ts)
            sr_s.append(hr)
            si_s.append(hi)
            hs = glu_proj(z, w_glu, hs)

        hp = ffn(hp, rmsnorm(hp, norm_ffn[i], BF16), ffn_w_gu_b[i], ffn_w_down_b[i])
        hs = ffn(hs, rmsnorm(hs, norm_ffn[i], BF16), ffn_w_gu_b[i], ffn_w_down_b[i])

    y_p = rmsnorm(hp, norm_final, F32).reshape(batch, t_len, d)
    y_s = rmsnorm(hs, norm_final, F32).reshape(bd, ts, d)
    return (y_p, y_s, jnp.stack(kv_p), jnp.stack(kv_s), jnp.stack(win_p), jnp.stack(win_s),
            jnp.stack(sr_p), jnp.stack(si_p), jnp.stack(sr_s), jnp.stack(si_s))
```

```python
import functools

import jax
import jax.numpy as jnp
from jax import lax
from jax.experimental import pallas as pl
from jax.experimental.pallas import tpu as pltpu

EPS = 1e-6
NEG_INF = -1e30
ROPE_THETA = 10000.0
CMP_STRIDE = 16
CMP_LEN = 2 * CMP_STRIDE
SEL_BLOCK = 64
N_SELECT = 16
WINDOW = 512
Q_BLK = 64

LANE = 128
SUBLANE = 8
VMEM_LIMIT_BYTES = 56 * 1024 * 1024

F32 = jnp.float32
BF16 = jnp.bfloat16
_NT = (((1,), (1,)), ((), ()))
LOG2_E = 1.4426950408889634


def _cparams(semantics):
    return pltpu.CompilerParams(dimension_semantics=semantics, vmem_limit_bytes=VMEM_LIMIT_BYTES)


def _tile(dim, pref, align=LANE):
    if dim <= pref:
        return dim
    t = (pref // align) * align
    while t >= align:
        if dim % t == 0:
            return t
        t -= align
    return dim


def _rmsnorm_kernel(x_ref, g_ref, o_ref):
    x = x_ref[...]
    y = x * lax.rsqrt(jnp.mean(x * x, axis=-1, keepdims=True) + EPS)
    o_ref[...] = (y * g_ref[...]).astype(o_ref.dtype)


def rmsnorm(x, g, out_dtype, time_major=None):
    m, d = x.shape
    if time_major is None:
        tm = _tile(m, 256, SUBLANE)
        out_spec = pl.BlockSpec((tm, d), lambda i: (i, 0))
        out_shape = (m, d)
    else:
        batch, t_len = time_major
        tm = _tile(t_len, 256, SUBLANE)
        nt = t_len // tm
        out_spec = pl.BlockSpec((tm, d), lambda i: (i % nt, i // nt))
        out_shape = (t_len, batch * d)
    out = pl.pallas_call(
        _rmsnorm_kernel,
        grid=(m // tm,),
        in_specs=[pl.BlockSpec((tm, d), lambda i: (i, 0)), pl.BlockSpec((1, d), lambda i: (0, 0))],
        out_specs=out_spec,
        out_shape=jax.ShapeDtypeStruct(out_shape, out_dtype),
        compiler_params=_cparams(("parallel",)),
        name="rmsnorm",
    )(x, g.reshape(1, d))
    return out.reshape(m, d)


def _cast_kernel(x_ref, o_ref):
    o_ref[...] = x_ref[...].astype(o_ref.dtype)


def cast_bf16(w):
    shape = w.shape
    w2 = w.reshape(-1, shape[-1])
    r, c = w2.shape
    tr = _tile(r, 256, SUBLANE)
    tc = c if c <= 8192 else _tile(c, 8192)
    out = pl.pallas_call(
        _cast_kernel,
        grid=(r // tr, c // tc),
        in_specs=[pl.BlockSpec((tr, tc), lambda i, j: (i, j))],
        out_specs=pl.BlockSpec((tr, tc), lambda i, j: (i, j)),
        out_shape=jax.ShapeDtypeStruct((r, c), BF16),
        compiler_params=_cparams(("parallel", "parallel")),
        name="cast_bf16",
    )(w2)
    return out.reshape(shape)


def _mm_kernel(*refs, tn, dual, epilogue, rope_cfg):
    it = iter(refs)
    a_ref = next(it)
    w_ref = next(it)
    w2_ref = next(it) if dual else None
    res_ref = next(it) if epilogue in ("residual", "sglu") else None
    cos_ref = next(it) if epilogue == "rope" else None
    sin_ref = next(it) if epilogue == "rope" else None
    o_ref = next(it)

    a = a_ref[...]
    acc = jnp.dot(a, w_ref[...], preferred_element_type=F32)
    if dual:
        acc2 = jnp.dot(a, w2_ref[...], preferred_element_type=F32)

    if epilogue == "residual":
        o_ref[...] = (res_ref[...] + acc).astype(o_ref.dtype)
    elif epilogue == "swiglu":
        o_ref[...] = (jax.nn.silu(acc) * acc2).astype(o_ref.dtype)
    elif epilogue == "sglu":
        o_ref[...] = (res_ref[...] + acc * jax.nn.sigmoid(acc2)).astype(o_ref.dtype)
    elif epilogue == "rope":
        q_width, kv_width, kvw, hd = rope_cfg
        col0 = pl.program_id(1) * tn
        in_kv = (col0 >= q_width) & (col0 < q_width + kv_width)
        is_rope = (col0 < q_width) | (in_kv & (((col0 - q_width) // kvw) % 2 == 0))

        @pl.when(is_rope)
        def _():
            cos = cos_ref[...]
            sin = sin_ref[...]
            for c in range(tn // hd):
                x = acc[:, c * hd:(c + 1) * hd]
                o_ref[:, c * hd:(c + 1) * hd] = x * cos + pltpu.roll(x, hd // 2, 1) * sin

        @pl.when(jnp.logical_not(is_rope))
        def _():
            o_ref[...] = acc


def matmul(a, w, *, tm, tn, epilogue, out_dtype=F32, n_out=None, w2_col_off=None, res=None, cos=None,
           sin=None, rope_cfg=None, a_buffers=2, a_time_major=None):
    m, kdim = a.shape
    n = w.shape[1] if n_out is None else n_out
    dual = epilogue in ("swiglu", "sglu")
    assert m % tm == 0
    a_spec = pl.BlockSpec((tm, kdim), lambda i, j: (i, 0))
    if a_buffers != 2:
        a_spec = pl.BlockSpec((tm, kdim), lambda i, j: (i, 0), pipeline_mode=pl.Buffered(a_buffers))
    if a_time_major is not None:
        batch, t_len = a_time_major
        assert t_len % tm == 0 and batch * t_len == m
        nt = t_len // tm
        a = a.reshape(t_len, batch * kdim)
        a_spec = pl.BlockSpec((tm, kdim), lambda i, j: (i % nt, i // nt))
    in_specs = [a_spec, pl.BlockSpec((kdim, tn), lambda i, j: (0, j))]
    args = [a, w]
    if dual:
        assert w2_col_off % tn == 0 and n % tn == 0
        off = w2_col_off // tn
        in_specs.append(pl.BlockSpec((kdim, tn), lambda i, j: (0, j + off)))
        args.append(w)
    if epilogue in ("residual", "sglu"):
        in_specs.append(pl.BlockSpec((tm, tn), lambda i, j: (i, j)))
        args.append(res)
    if epilogue == "rope":
        hd = rope_cfg[3]
        in_specs += [pl.BlockSpec((tm, hd), lambda i, j: (i, 0))] * 2
        args += [cos, sin]
    return pl.pallas_call(
        functools.partial(_mm_kernel, tn=tn, dual=dual, epilogue=epilogue, rope_cfg=rope_cfg),
        grid=(m // tm, pl.cdiv(n, tn)),
        in_specs=in_specs,
        out_specs=pl.BlockSpec((tm, tn), lambda i, j: (i, j)),
        out_shape=jax.ShapeDtypeStruct((m, n), out_dtype),
        compiler_params=_cparams(("parallel", "arbitrary")),
        name="mm_" + epilogue,
    )(*args)


def _compress_kernel(x_ref, pe_ref, w1_ref, w2_ref, o_ref, w1_s, w2_s):
    @pl.when(pl.program_id(0) == 0)
    def _():
        w1_s[...] = w1_ref[...].astype(BF16)
        w2_s[...] = w2_ref[...].astype(BF16)

    x = (x_ref[...] + pe_ref[...]).astype(BF16)
    h = jax.nn.gelu(jnp.dot(x, w1_s[...], preferred_element_type=F32))
    o_ref[...] = jnp.dot(h.astype(BF16), w2_s[...], preferred_element_type=F32).astype(o_ref.dtype)


def compress_rows(flat, pe, w1, w2):
    r, kdim = flat.shape
    hid, hd = w2.shape
    tr = _tile(r, 512, SUBLANE)
    return pl.pallas_call(
        _compress_kernel,
        grid=(r // tr,),
        in_specs=[pl.BlockSpec((tr, kdim), lambda i: (i, 0)),
                  pl.BlockSpec((1, kdim), lambda i: (0, 0)),
                  pl.BlockSpec((kdim, hid), lambda i: (0, 0)),
                  pl.BlockSpec((hid, hd), lambda i: (0, 0))],
        out_specs=pl.BlockSpec((tr, hd), lambda i: (i, 0)),
        out_shape=jax.ShapeDtypeStruct((r, hd), BF16),
        scratch_shapes=[pltpu.VMEM((kdim, hid), BF16), pltpu.VMEM((hid, hd), BF16)],
        compiler_params=_cparams(("arbitrary",)),
        name="compress",
    )(flat, pe.reshape(1, kdim), w1, w2)


def _cmp_flat(x, ncp):
    b, tk, n_kv, hd = x.shape
    n_chunk = tk // CMP_STRIDE
    c = x[:, :n_chunk * CMP_STRIDE].reshape(b, n_chunk, CMP_STRIDE, n_kv, hd)
    blk = jnp.concatenate([c[:, :-1], c[:, 1:]], axis=2)
    flat = blk.transpose(0, 3, 1, 2, 4).reshape(b, n_kv, n_chunk - 1, CMP_LEN * hd)
    return jnp.pad(flat, ((0, 0), (0, 0), (0, ncp - (n_chunk - 1)), (0, 0)))


def _overlap_matrix(ncp, nsp):
    ic = jnp.arange(ncp, dtype=jnp.int32)[:, None] * CMP_STRIDE
    js = jnp.arange(nsp, dtype=jnp.int32)[None, :] * SEL_BLOCK
    return ((ic < js + SEL_BLOCK) & (ic + CMP_LEN > js)).astype(F32)


def _compress_paged_kernel(*refs, pg, page, hd, n_kv, n_groups, n_chunk, ncp):
    pt_ref = refs[0]
    page_refs = refs[1:1 + pg]
    pe_ref, w1_ref, w2_ref, o_ref, la_s, lb_s, tail_s = refs[1 + pg:]
    del pt_ref, n_groups
    g = pl.program_id(1)
    cpp = page // CMP_STRIDE
    rows_g = pg * cpp
    half = CMP_STRIDE * hd
    tail = 2 * SUBLANE

    if ncp > n_chunk:
        @pl.when(g == 0)
        def _():
            o_ref[...] = jnp.zeros_like(o_ref)

    for i in range(pg):
        for r in range(CMP_STRIDE):
            x = page_refs[i][pl.ds(r, cpp, stride=CMP_STRIDE), :, :]
            xt = jnp.swapaxes(x, 0, 1)
            for j in range(2 * n_kv):
                s, k = divmod(j, n_kv)
                rows = slice(k * rows_g + i * cpp, k * rows_g + (i + 1) * cpp)
                la_s[s, rows, r * hd:(r + 1) * hd] = xt[j] + pe_ref[s, r:r + 1, :]
                lb_s[s, rows, r * hd:(r + 1) * hd] = xt[j] + pe_ref[s, CMP_STRIDE + r:CMP_STRIDE + r + 1, :]

    row0 = pl.multiple_of(g * rows_g, rows_g)
    last = lax.broadcasted_iota(jnp.int32, (tail, 1), 0) == tail - 1
    for s in range(2):
        ca = jnp.dot(la_s[s].astype(BF16), w1_ref[s, :half, :], preferred_element_type=F32)
        cb = jnp.dot(lb_s[s].astype(BF16), w1_ref[s, half:, :], preferred_element_type=F32)
        w2 = w2_ref[s].astype(BF16)
        for k in range(n_kv):
            j = s * n_kv + k
            ca_k = ca[k * rows_g:(k + 1) * rows_g]
            cb_k = cb[k * rows_g:(k + 1) * rows_g]

            @pl.when(g > 0)
            def _(j=j, s=s, k=k, cb_k=cb_k, w2=w2):
                h = jax.nn.gelu(tail_s[j] + cb_k[0:1, :])
                o_new = jnp.dot(h.astype(BF16), w2, preferred_element_type=F32)
                prev = pl.ds(row0 - tail, tail)
                o_old = o_ref[s, k, prev, :].astype(F32)
                o_ref[s, k, prev, :] = jnp.where(last, o_new, o_old).astype(o_ref.dtype)

            h = jax.nn.gelu(ca_k + pltpu.roll(cb_k, rows_g - 1, 0))
            o_ref[s, k, pl.ds(row0, rows_g), :] = jnp.dot(h.astype(BF16), w2,
                                                          preferred_element_type=F32).astype(o_ref.dtype)
            tail_s[j] = ca_k[rows_g - tail:, :]


def compress_paged(page_table, cache4, pe, w1, w2, *, n_kv, hd, ncp):
    bd, n_pages = page_table.shape
    page = cache4.shape[1]
    assert page % CMP_STRIDE == 0
    cpp = page // CMP_STRIDE
    n_chunk = n_pages * cpp
    assert ncp >= n_chunk and cpp == SUBLANE and 2 * n_kv == SUBLANE
    pg = _tile(n_pages, 16, 1)
    n_groups = n_pages // pg
    hid = w1.shape[-1]

    def page_spec(i):
        return pl.BlockSpec((None, page, 2 * n_kv, hd), lambda b, g, pt: (pt[b * n_pages + g * pg + i], 0, 0, 0))

    kern = functools.partial(_compress_paged_kernel, pg=pg, page=page, hd=hd, n_kv=n_kv, n_groups=n_groups,
                             n_chunk=n_chunk, ncp=ncp)
    grid_spec = pltpu.PrefetchScalarGridSpec(
        num_scalar_prefetch=1,
        grid=(bd, n_groups),
        in_specs=[page_spec(i) for i in range(pg)] + [
            pl.BlockSpec((2, CMP_LEN, hd), lambda b, g, pt: (0, 0, 0)),
            pl.BlockSpec((2, CMP_LEN * hd, hid), lambda b, g, pt: (0, 0, 0)),
            pl.BlockSpec((2, hid, hd), lambda b, g, pt: (0, 0, 0))],
        out_specs=pl.BlockSpec((2, None, n_kv, ncp, hd), lambda b, g, pt: (0, b, 0, 0, 0)),
        scratch_shapes=[pltpu.VMEM((2, n_kv * pg * cpp, CMP_STRIDE * hd), F32),
                        pltpu.VMEM((2, n_kv * pg * cpp, CMP_STRIDE * hd), F32),
                        pltpu.VMEM((2 * n_kv, 2 * SUBLANE, hid), F32)],
    )
    return pl.pallas_call(
        kern,
        grid_spec=grid_spec,
        out_shape=jax.ShapeDtypeStruct((2, bd, n_kv, ncp, hd), BF16),
        compiler_params=_cparams(("arbitrary", "arbitrary")),
        name="compress_paged",
    )(page_table.reshape(-1), *([cache4] * pg), pe, w1, w2)


def _nsa_prompt_kernel(q_ref, ks_ref, vs_ref, kw_ref, vw_ref, kc_ref, vc_ref, gate_ref, ovl_ref, blk_ref,
                       o_ref, ks_s, vs_s, kw_s, vw_s, acc_s, mx_s, *, gqa, hd, t_len, n_cmp, n_sel, k_sel,
                       sel_ext, scale):
    qb = pl.program_id(2)

    @pl.when(qb == 0)
    def _():
        ks_s[:, :hd] = ks_ref[...].astype(BF16)
        ks_s[:, hd:] = blk_ref[...]
        vs_s[:, :hd] = vs_ref[...].astype(BF16)
        vs_s[:, hd:] = jnp.ones((t_len, LANE), BF16)
        kw_s[...] = kw_ref[...].astype(BF16)
        vw_s[:, :hd] = vw_ref[...].astype(BF16)
        vw_s[:, hd:] = jnp.ones((t_len, LANE), BF16)

    rows = gqa * Q_BLK
    b0 = qb * Q_BLK
    q = q_ref[...]
    qg = jnp.concatenate([q[:, g * hd:(g + 1) * hd] for g in range(gqa)], axis=0).astype(BF16)
    pos_q = b0 + lax.broadcasted_iota(jnp.int32, (Q_BLK, 1), 0)

    c_exp = scale * LOG2_E

    def exp_shifted(s3):
        return jnp.exp2((s3 - jnp.max(s3, axis=-1, keepdims=True)) * c_exp)

    ncp = kc_ref.shape[0]
    s = lax.dot_general(qg, kc_ref[...], _NT, preferred_element_type=F32)
    ci = lax.broadcasted_iota(jnp.int32, (1, ncp), 1)
    m_cmp = ((ci * CMP_STRIDE + (CMP_LEN - 1)) <= pos_q) & (ci < n_cmp)
    s3 = s.reshape(gqa, Q_BLK, ncp) + jnp.where(m_cmp, 0.0, NEG_INF)[None]
    e3 = exp_shifted(s3) * m_cmp.astype(F32)[None]
    l3 = jnp.sum(e3, axis=-1, keepdims=True)
    p3 = e3 * jnp.where(l3 > 0.0, 1.0 / l3, 0.0)
    o_cmp = jnp.dot(p3.reshape(rows, ncp).astype(BF16), vc_ref[...], preferred_element_type=F32)

    nsp = ovl_ref.shape[1]
    imp = jnp.dot(jnp.sum(p3, axis=0), ovl_ref[...], preferred_element_type=F32,
                  precision=lax.Precision.HIGHEST)
    jj = lax.broadcasted_iota(jnp.int32, (1, nsp), 1)
    cur = pos_q // SEL_BLOCK
    forced = (jj == 0) | (jj == cur) | (jj == cur - 1)
    future = jj * SEL_BLOCK > pos_q
    score = jnp.where(future, -jnp.inf, jnp.where(forced, jnp.inf, imp))
    rank = jnp.zeros((Q_BLK, nsp), F32)
    for j2 in range(n_sel):
        col = score[:, j2:j2 + 1]
        beats = (col > score) | ((col == score) & (j2 < jj))
        rank = rank + beats.astype(F32)
    sel_ok = (rank < k_sel) & (jj < qb)
    nb = jnp.where(sel_ok, 0.0, NEG_INF).astype(BF16)
    q_aug = jnp.concatenate([qg, jnp.concatenate([nb] * gqa, axis=0)], axis=1)

    tq = lax.broadcasted_iota(jnp.int32, (Q_BLK, 1), 0)
    bias_d = jnp.where(lax.broadcasted_iota(jnp.int32, (1, SEL_BLOCK), 1) <= tq, 0.0, NEG_INF)
    diag = pl.ds(pl.multiple_of(b0, SEL_BLOCK), SEL_BLOCK)
    s_d = (lax.dot_general(qg, ks_s[diag, :hd], _NT, preferred_element_type=F32)
           + jnp.concatenate([bias_d] * gqa, axis=0))
    m_d = jnp.max(s_d, axis=-1, keepdims=True)

    def prefix(kv_len):
        s = lax.dot_general(q_aug, ks_s[:kv_len, :], _NT, preferred_element_type=F32)
        mx = jnp.maximum(jnp.max(s, axis=-1, keepdims=True), m_d)
        e = jnp.exp2((s - mx) * c_exp).astype(BF16)
        acc_s[...] = jnp.dot(e, vs_s[:kv_len, :], preferred_element_type=F32)
        mx_s[...] = mx

    def no_prefix():
        acc_s[...] = jnp.zeros_like(acc_s)
        mx_s[...] = m_d

    n_ext = (b0 + sel_ext - 1) // sel_ext
    pl.when(n_ext == 0)(no_prefix)
    for c in range(1, t_len // sel_ext + 1):
        pl.when(n_ext == c)(functools.partial(prefix, c * sel_ext))
    e_d = jnp.exp2((s_d - mx_s[...]) * c_exp).astype(BF16)
    acc = acc_s[...] + jnp.dot(e_d, vs_s[diag, :], preferred_element_type=F32)
    o_sel = acc[:, :hd] / acc[:, hd:hd + 1]

    wk = WINDOW + Q_BLK
    start = pl.multiple_of(jnp.maximum(b0 - WINDOW, 0), Q_BLK)
    s = lax.dot_general(qg, kw_s[pl.ds(start, wk), :], _NT, preferred_element_type=F32)
    dist = pos_q - (start + lax.broadcasted_iota(jnp.int32, (1, wk), 1))
    m_win = (dist >= 0) & (dist <= WINDOW)
    s3 = s.reshape(gqa, Q_BLK, wk) + jnp.where(m_win, 0.0, NEG_INF)[None]
    e3 = exp_shifted(s3)
    acc_w = jnp.dot(e3.reshape(rows, wk).astype(BF16), vw_s[pl.ds(start, wk), :], preferred_element_type=F32)
    o_win = acc_w[:, :hd] / acc_w[:, hd:hd + 1]

    gate = jax.nn.sigmoid(gate_ref[...])
    for g in range(gqa):
        r = slice(g * Q_BLK, (g + 1) * Q_BLK)
        o = (gate[:, 3 * g:3 * g + 1] * o_cmp[r] + gate[:, 3 * g + 1:3 * g + 2] * o_sel[r]
             + gate[:, 3 * g + 2:3 * g + 3] * o_win[r])
        o_ref[:, g * hd:(g + 1) * hd] = o.astype(o_ref.dtype)


def nsa_prompt_attention(proj, gates_r, kc, vc, *, batch, t_len, n_kv, gqa, hd, n_cmp):
    m = batch * t_len
    nq = t_len // Q_BLK
    q_width = n_kv * gqa * hd
    kvw = n_kv * hd
    ncp = kc.shape[2]
    n_sel = -(-t_len // SEL_BLOCK)
    nsp = -(-n_sel // LANE) * LANE
    k_sel = min(N_SELECT, n_sel)
    assert t_len % Q_BLK == 0 and t_len >= WINDOW + Q_BLK and Q_BLK == SEL_BLOCK
    ovl = _overlap_matrix(ncp, nsp)
    blk_onehot = (jnp.arange(t_len, dtype=jnp.int32)[:, None] // SEL_BLOCK
                  == jnp.arange(nsp, dtype=jnp.int32)[None, :]).astype(BF16)

    def kv_spec(slot):
        base = (q_width + slot * kvw) // hd
        return pl.BlockSpec((t_len, hd), lambda b, k, i: (b, base + k))

    sel_ext = _tile(t_len, max(t_len // 4, LANE))
    kern = functools.partial(_nsa_prompt_kernel, gqa=gqa, hd=hd, t_len=t_len, n_cmp=n_cmp, n_sel=n_sel,
                             k_sel=k_sel, sel_ext=sel_ext, scale=hd ** -0.5)
    return pl.pallas_call(
        kern,
        grid=(batch, n_kv, nq),
        in_specs=[pl.BlockSpec((Q_BLK, gqa * hd), lambda b, k, i: (b * nq + i, k)),
                  kv_spec(2), kv_spec(3), kv_spec(4), kv_spec(5),
                  pl.BlockSpec((None, None, ncp, hd), lambda b, k, i: (b, k, 0, 0)),
                  pl.BlockSpec((None, None, ncp, hd), lambda b, k, i: (b, k, 0, 0)),
                  pl.BlockSpec((None, Q_BLK, 3 * gqa), lambda b, k, i: (k, b * nq + i, 0)),
                  pl.BlockSpec((ncp, nsp), lambda b, k, i: (0, 0)),
                  pl.BlockSpec((t_len, nsp), lambda b, k, i: (0, 0))],
        out_specs=pl.BlockSpec((Q_BLK, gqa * hd), lambda b, k, i: (b * nq + i, k)),
        out_shape=jax.ShapeDtypeStruct((m, q_width), BF16),
        scratch_shapes=[pltpu.VMEM((t_len, hd + nsp), BF16), pltpu.VMEM((t_len, hd + LANE), BF16),
                        pltpu.VMEM((t_len, hd), BF16), pltpu.VMEM((t_len, hd + LANE), BF16),
                        pltpu.VMEM((gqa * Q_BLK, hd + LANE), F32), pltpu.VMEM((gqa * Q_BLK, 1), F32)],
        compiler_params=_cparams(("parallel", "parallel", "arbitrary")),
        name="nsa_prompt",
    )(proj, proj, proj, proj, proj, kc, vc, gates_r, ovl, blk_onehot)


def _nsa_dec_select_kernel(q_ref, kc_ref, vc_ref, ovl_ref, ocmp_ref, idx_ref, *, n_cmp, n_sel, pos_q, scale):
    q = q_ref[...].astype(BF16)
    ncp = kc_ref.shape[0]
    nsp = ovl_ref.shape[1]
    s = lax.dot_general(q, kc_ref[...], _NT, preferred_element_type=F32) * scale
    ci = lax.broadcasted_iota(jnp.int32, (1, ncp), 1)
    m_cmp = ((ci * CMP_STRIDE + (CMP_LEN - 1)) <= pos_q) & (ci < n_cmp)
    s = jnp.where(m_cmp, s, NEG_INF)
    e = jnp.where(m_cmp, jnp.exp(s - jnp.max(s, axis=-1, keepdims=True)), 0.0)
    l = jnp.sum(e, axis=-1, keepdims=True)
    p = e * jnp.where(l > 0.0, 1.0 / l, 0.0)
    ocmp_ref[...] = jnp.dot(p.astype(BF16), vc_ref[...], preferred_element_type=F32)

    imp = jnp.sum(jnp.dot(p, ovl_ref[...], preferred_element_type=F32, precision=lax.Precision.HIGHEST),
                  axis=0, keepdims=True)
    jj = lax.broadcasted_iota(jnp.int32, (1, nsp), 1)
    cur = pos_q // SEL_BLOCK
    forced = (jj == 0) | (jj == cur) | (jj == cur - 1)
    future = jj * SEL_BLOCK > pos_q
    score = jnp.where(future, -jnp.inf, jnp.where(forced, jnp.inf, imp))
    rr = lax.broadcasted_iota(jnp.int32, (nsp, nsp), 0)
    cc = lax.broadcasted_iota(jnp.int32, (nsp, nsp), 1)
    score_b = jnp.broadcast_to(score, (nsp, nsp))
    score_col = jnp.sum(jnp.where(rr == cc, score_b, 0.0), axis=1, keepdims=True)
    beaten = ((score_b > score_col) | ((score_b == score_col) & (cc < rr))) & (cc < n_sel)
    rank_col = jnp.sum(beaten.astype(F32), axis=1, keepdims=True)
    lane = lax.broadcasted_iota(jnp.int32, (nsp, LANE), 1)
    row = lax.broadcasted_iota(jnp.int32, (nsp, LANE), 0)
    hit = (rank_col == lane.astype(F32)) & (row < n_sel)
    idx_ref[...] = jnp.sum(jnp.where(hit, row.astype(F32), 0.0), axis=0, keepdims=True).astype(jnp.int32)


def nsa_dec_select(q, kc, vc, *, n_cmp, n_sel, pos_q):
    bd, n_kv, gqa, hd = q.shape
    ncp = kc.shape[2]
    nsp = -(-n_sel // LANE) * LANE
    kern = functools.partial(_nsa_dec_select_kernel, n_cmp=n_cmp, n_sel=n_sel, pos_q=pos_q, scale=hd ** -0.5)
    return pl.pallas_call(
        kern,
        grid=(bd, n_kv),
        in_specs=[pl.BlockSpec((None, None, gqa, hd), lambda b, k: (b, k, 0, 0)),
                  pl.BlockSpec((None, None, ncp, hd), lambda b, k: (b, k, 0, 0)),
                  pl.BlockSpec((None, None, ncp, hd), lambda b, k: (b, k, 0, 0)),
                  pl.BlockSpec((ncp, nsp), lambda b, k: (0, 0))],
        out_specs=[pl.BlockSpec((None, None, gqa, hd), lambda b, k: (b, k, 0, 0)),
                   pl.BlockSpec((None, None, 1, LANE), lambda b, k: (b, k, 0, 0))],
        out_shape=[jax.ShapeDtypeStruct((bd, n_kv, gqa, hd), F32),
                   jax.ShapeDtypeStruct((bd, n_kv, 1, LANE), jnp.int32)],
        compiler_params=_cparams(("parallel", "parallel")),
        name="nsa_dec_select",
    )(q, kc, vc, _overlap_matrix(ncp, nsp))


def _nsa_dec_attend_kernel(*refs, k_sel, n_kv, past_len, pos_q, scale):
    pt_ref, idx_ref, q_ref = refs[:3]
    kv_refs = refs[3:3 + k_sel]
    new_ref, kw_ref, vw_ref, ocmp_ref, gate_ref, o_ref = refs[3 + k_sel:]
    del pt_ref
    b = pl.program_id(0)
    k = pl.program_id(1)

    q = q_ref[...].astype(BF16)
    new = new_ref[...].astype(BF16)
    newf = new.astype(F32)
    s_all = lax.dot_general(q, new, _NT, preferred_element_type=F32) * scale

    first = (lax.broadcasted_iota(jnp.int32, (SEL_BLOCK, 1), 0) == 0).astype(F32)
    new_k = first * new_ref[2:3, :]
    new_v = first * new_ref[3:4, :]
    off = lax.broadcasted_iota(jnp.int32, (1, SEL_BLOCK), 1)
    s_blk, v_blk = [], []
    for r in range(k_sel):
        tok0 = idx_ref[(b * n_kv + k) * k_sel + r] * SEL_BLOCK
        is_new = tok0 >= past_len
        kb = jnp.where(is_new, new_k, kv_refs[r][:, k, :]).astype(BF16)
        v_blk.append(jnp.where(is_new, new_v, kv_refs[r][:, n_kv + k, :]).astype(BF16))
        s = lax.dot_general(q, kb, _NT, preferred_element_type=F32) * scale
        s_blk.append(jnp.where(tok0 + off <= pos_q, s, NEG_INF))
    mx = functools.reduce(jnp.maximum, [jnp.max(s, axis=-1, keepdims=True) for s in s_blk])
    l_sel = jnp.zeros_like(mx)
    acc = jnp.zeros((q.shape[0], q.shape[1]), F32)
    for s, vb in zip(s_blk, v_blk):
        e = jnp.exp(s - mx)
        l_sel = l_sel + jnp.sum(e, axis=-1, keepdims=True)
        acc = acc + jnp.dot(e.astype(BF16), vb, preferred_element_type=F32)
    o_sel = acc / l_sel

    wb = kw_ref.shape[0]
    s = lax.dot_general(q, kw_ref[...].astype(BF16), _NT, preferred_element_type=F32) * scale
    pos_w = (past_len - wb) + lax.broadcasted_iota(jnp.int32, (1, wb), 1)
    dist = pos_q - pos_w
    m_win = (dist >= 0) & (dist <= WINDOW) & (pos_w >= 0)
    s = jnp.where(m_win, s, NEG_INF)
    s_nw = s_all[:, 4:5]
    mx = jnp.maximum(jnp.max(s, axis=-1, keepdims=True), s_nw)
    e = jnp.where(m_win, jnp.exp(s - mx), 0.0)
    e_nw = jnp.exp(s_nw - mx)
    l_w = jnp.sum(e, axis=-1, keepdims=True) + e_nw
    o_win = (jnp.dot(e.astype(BF16), vw_ref[...].astype(BF16), preferred_element_type=F32)
             + e_nw.astype(BF16).astype(F32) * newf[5:6, :]) / l_w

    gate = jax.nn.sigmoid(gate_ref[...])
    o = gate[:, 0:1] * ocmp_ref[...] + gate[:, 1:2] * o_sel + gate[:, 2:3] * o_win
    o_ref[...] = o.astype(o_ref.dtype)


def nsa_dec_attend(page_table, idx, q, cache4, new_rows, win2d, o_cmp, gates, *, k_sel, past_len, pos_q):
    bd, n_kv, gqa, hd = q.shape
    n_pages = page_table.shape[1]
    page = cache4.shape[1]
    wb = win2d.shape[1]
    assert page % SEL_BLOCK == 0 and past_len % SEL_BLOCK == 0 and pos_q == past_len
    assert 2 * n_kv == SUBLANE and cache4.shape[2] == 4 * n_kv
    bpp = page // SEL_BLOCK
    kern = functools.partial(_nsa_dec_attend_kernel, k_sel=k_sel, n_kv=n_kv, past_len=past_len, pos_q=pos_q,
                             scale=hd ** -0.5)

    def blk_spec(r):
        def index_map(b, k, pt, ix):
            j = ix[(b * n_kv + k) * k_sel + r]
            return pt[b * n_pages + jnp.minimum(j // bpp, n_pages - 1)], j % bpp, 1, 0

        return pl.BlockSpec((None, SEL_BLOCK, 2 * n_kv, hd), index_map)

    grid_spec = pltpu.PrefetchScalarGridSpec(
        num_scalar_prefetch=2,
        grid=(bd, n_kv),
        in_specs=[pl.BlockSpec((None, None, gqa, hd), lambda b, k, pt, ix: (b, k, 0, 0))]
        + [blk_spec(r) for r in range(k_sel)]
        + [pl.BlockSpec((None, None, SUBLANE, hd), lambda b, k, pt, ix: (b, k, 0, 0)),
           pl.BlockSpec((None, wb, hd), lambda b, k, pt, ix: (b, 0, k)),
           pl.BlockSpec((None, wb, hd), lambda b, k, pt, ix: (b, 0, n_kv + k)),
           pl.BlockSpec((None, None, gqa, hd), lambda b, k, pt, ix: (b, k, 0, 0)),
           pl.BlockSpec((None, None, gqa, 3), lambda b, k, pt, ix: (b, k, 0, 0))],
        out_specs=pl.BlockSpec((None, None, gqa, hd), lambda b, k, pt, ix: (b, k, 0, 0)),
    )
    return pl.pallas_call(
        kern,
        grid_spec=grid_spec,
        out_shape=jax.ShapeDtypeStruct((bd, n_kv, gqa, hd), BF16),
        compiler_params=_cparams(("arbitrary", "arbitrary")),
        name="nsa_dec_attend",
    )(page_table.reshape(-1), idx, q, *([cache4] * k_sel), new_rows, win2d, win2d, o_cmp, gates)


N_POW = SUBLANE


def _s5_disc_kernel(lre_ref, lim_ref, ldt_ref, bre_ref, bim_ref, bbre_ref, bbim_ref, pre_ref, pim_ref):
    lam_re = lre_ref[...]
    lam_im = lim_ref[...]
    dt = jnp.exp(ldt_ref[...])
    mag = jnp.exp(lam_re * dt)
    ab_re = mag * jnp.cos(lam_im * dt)
    ab_im = mag * jnp.sin(lam_im * dt)
    nr, ni = ab_re - 1.0, ab_im
    den = lam_re * lam_re + lam_im * lam_im
    f_re = (nr * lam_re + ni * lam_im) / den
    f_im = (ni * lam_re - nr * lam_im) / den
    b_re = bre_ref[...]
    b_im = bim_ref[...]
    bbre_ref[...] = f_re * b_re - f_im * b_im
    bbim_ref[...] = f_re * b_im + f_im * b_re
    pr, pi = ab_re, ab_im
    pre_ref[0] = pr
    pim_ref[0] = pi
    for n in range(1, N_POW):
        pr, pi = pr * ab_re - pi * ab_im, pr * ab_im + pi * ab_re
        pre_ref[n] = pr
        pim_ref[n] = pi


def s5_discretize(lam_re, lam_im, log_dt, b_re, b_im):
    g, n, c = b_re.shape
    rep = lambda x: jnp.broadcast_to(x[:, None, :], (g, c, n)).reshape(g * c, n)
    ldt = jnp.broadcast_to(log_dt[:, None, None], (g, c, n)).reshape(g * c, n)
    bt = lambda x: x.transpose(0, 2, 1).reshape(g * c, n)
    full = pl.BlockSpec((g * c, n), lambda: (0, 0))
    pw = pl.BlockSpec((N_POW, g * c, n), lambda: (0, 0, 0))
    bbre, bbim, pre, pim = pl.pallas_call(
        _s5_disc_kernel,
        in_specs=[full] * 5,
        out_specs=[full, full, pw, pw],
        out_shape=[jax.ShapeDtypeStruct((g * c, n), F32)] * 2
        + [jax.ShapeDtypeStruct((N_POW, g * c, n), F32)] * 2,
        compiler_params=pltpu.CompilerParams(vmem_limit_bytes=VMEM_LIMIT_BYTES),
        name="s5_discretize",
    )(rep(lam_re), rep(lam_im), ldt, bt(b_re), bt(b_im))
    pick = lambda x: x.reshape(N_POW, g, c, n)[:, :, 0, :]
    return bbre.reshape(g, c, n), bbim.reshape(g, c, n), pick(pre), pick(pim)


def _s5_scan_kernel(u_ref, h0r_ref, h0i_ref, bbr_ref, bbi_ref, ccr_ref, cci_ref, d_ref, mr_ref, mi_ref,
                    pr_ref, pi_ref, z_ref, hr_ref, hi_ref, xr_s, xi_s, cr_s, ci_s, *, batch, n_groups, n_t):
    t = pl.program_id(1)
    shifts = [batch << n for n in range(8) if (batch << n) < SUBLANE]
    row = lax.broadcasted_iota(jnp.int32, (SUBLANE, 1), 0)

    @pl.when(t == 0)
    def _():
        cr_s[...] = h0r_ref[...]
        ci_s[...] = h0i_ref[...]

    u = u_ref[...]
    ub = u.astype(BF16)
    xr_s[...] = jnp.dot(ub, bbr_ref[...].astype(BF16), preferred_element_type=F32)
    xi_s[...] = jnp.dot(ub, bbi_ref[...].astype(BF16), preferred_element_type=F32)
    pr = pr_ref[...]
    pi = pi_ref[...]

    def group(r, carry):
        cr, ci = carry
        rows = pl.ds(pl.multiple_of(r * SUBLANE, SUBLANE), SUBLANE)
        xr = xr_s[rows, :]
        xi = xi_s[rows, :]
        for n, sh in enumerate(shifts):
            mr = mr_ref[n]
            mi = mi_ref[n]
            sr = pltpu.roll(xr, sh, 0)
            si = pltpu.roll(xi, sh, 0)
            xr, xi = xr + (mr * sr - mi * si), xi + (mr * si + mi * sr)
        xr, xi = xr + (pr * cr - pi * ci), xi + (pr * ci + pi * cr)
        xr_s[rows, :] = xr
        xi_s[rows, :] = xi
        cr, ci = xr, xi
        for sh in shifts:
            keep = row >= SUBLANE - sh
            cr = jnp.where(keep, cr, pltpu.roll(cr, SUBLANE - sh, 0))
            ci = jnp.where(keep, ci, pltpu.roll(ci, SUBLANE - sh, 0))
        return cr, ci

    cr, ci = lax.fori_loop(0, n_groups, group, (cr_s[...], ci_s[...]))
    cr_s[...] = cr
    ci_s[...] = ci

    y = (jnp.dot(xr_s[...].astype(BF16), ccr_ref[...].astype(BF16), preferred_element_type=F32)
         - jnp.dot(xi_s[...].astype(BF16), cci_ref[...].astype(BF16), preferred_element_type=F32)
         + d_ref[...] * u)
    z_ref[...] = jax.nn.gelu(y).astype(z_ref.dtype)

    @pl.when(t == n_t - 1)
    def _():
        hr_ref[...] = cr
        hi_ref[...] = ci


def s5_scan(u_tb, h0_re, h0_im, prm, *, batch, t_len):
    bb_re, bb_im, cc_re, cc_im, d_skip, pw_re, pw_im = prm
    n_slab, cw, sw = bb_re.shape
    assert SUBLANE % batch == 0
    tpg = SUBLANE // batch
    assert t_len % tpg == 0
    tt = _tile(t_len, 128, tpg)
    n_t = t_len // tt
    n_groups = tt // tpg
    shifts = [batch << n for n in range(8) if (batch << n) < SUBLANE]
    row = jnp.arange(SUBLANE)[None, :, None]

    def shift_tab(pw):
        tabs = [jnp.where(row >= sh, pw[sh // batch - 1][:, None, :], 0.0) for sh in shifts]
        return jnp.stack(tabs, axis=1) if tabs else jnp.zeros((n_slab, 1, SUBLANE, sw), F32)

    carry_tab = lambda pw: pw[jnp.arange(SUBLANE) // batch].transpose(1, 0, 2)
    tile_rows = lambda h: jnp.tile(h.reshape(batch, n_slab, sw).transpose(1, 0, 2), (1, tpg, 1))
    n_tab = max(len(shifts), 1)

    slab3 = lambda r, c: pl.BlockSpec((None, r, c), lambda s, t: (s, 0, 0))
    tab4 = pl.BlockSpec((None, n_tab, SUBLANE, sw), lambda s, t: (s, 0, 0, 0))
    kern = functools.partial(_s5_scan_kernel, batch=batch, n_groups=n_groups, n_t=n_t)
    z, hr, hi = pl.pallas_call(
        kern,
        grid=(n_slab, n_t),
        in_specs=[pl.BlockSpec((tt * batch, cw), lambda s, t: (t, s)),
                  slab3(SUBLANE, sw), slab3(SUBLANE, sw),
                  slab3(cw, sw), slab3(cw, sw), slab3(sw, cw), slab3(sw, cw),
                  pl.BlockSpec((1, cw), lambda s, t: (0, s)),
                  tab4, tab4,
                  slab3(SUBLANE, sw), slab3(SUBLANE, sw)],
        out_specs=[pl.BlockSpec((tt * batch, cw), lambda s, t: (t, s)), slab3(SUBLANE, sw), slab3(SUBLANE, sw)],
        out_shape=[jax.ShapeDtypeStruct((t_len * batch, n_slab * cw), BF16),
                   jax.ShapeDtypeStruct((n_slab, SUBLANE, sw), F32),
                   jax.ShapeDtypeStruct((n_slab, SUBLANE, sw), F32)],
        scratch_shapes=[pltpu.VMEM((tt * batch, sw), F32), pltpu.VMEM((tt * batch, sw), F32),
                        pltpu.VMEM((SUBLANE, sw), F32), pltpu.VMEM((SUBLANE, sw), F32)],
        compiler_params=_cparams(("parallel", "arbitrary")),
        name="s5_scan",
    )(u_tb, tile_rows(h0_re), tile_rows(h0_im), bb_re, bb_im, cc_re, cc_im, d_skip.reshape(1, -1),
      shift_tab(pw_re), shift_tab(pw_im), carry_tab(pw_re), carry_tab(pw_im))
    final = lambda h: h[:, :batch].transpose(1, 0, 2).reshape(h0_re.shape)
    return z, final(hr), final(hi)


def s5_prepare(lam_re, lam_im, log_dt, b_re, b_im, c_re, c_im, d_skip):
    g, n, c = b_re.shape
    gps = LANE // c
    n_slab = g // gps
    sw = gps * n
    bbt_re, bbt_im, pw_re, pw_im = s5_discretize(lam_re, lam_im, log_dt, b_re, b_im)
    eye = jnp.eye(gps, dtype=F32)

    def blockdiag_in(bt):
        x = bt.reshape(n_slab, gps, c, 1, n) * eye[None, :, None, :, None]
        return x.reshape(n_slab, gps * c, sw)

    def blockdiag_out(cm):
        x = cm.transpose(0, 2, 1).reshape(n_slab, gps, n, 1, c) * eye[None, :, None, :, None]
        return x.reshape(n_slab, sw, gps * c)

    return (blockdiag_in(bbt_re), blockdiag_in(bbt_im), blockdiag_out(c_re.astype(F32)),
            blockdiag_out(c_im.astype(F32)), d_skip.astype(F32), pw_re.reshape(N_POW, n_slab, sw),
            pw_im.reshape(N_POW, n_slab, sw))


MM_ROWS = 1024


def _rope_tables(pos, hd):
    half = hd // 2
    inv = ROPE_THETA ** (-jnp.arange(half, dtype=F32) / half)
    ang = pos.astype(F32)[:, None] * inv[None, :]
    cos = jnp.cos(ang)
    sin = jnp.sin(ang)
    return jnp.concatenate([cos, cos], axis=-1), jnp.concatenate([-sin, sin], axis=-1)


def in_proj(u, w_in, pos, *, q_width, kv_width, kvw, hd):
    m = u.shape[0]
    cos, sin = _rope_tables(pos, hd)
    return matmul(u, w_in, tm=_tile(m, MM_ROWS, SUBLANE), tn=kvw, epilogue="rope", cos=cos, sin=sin,
                  rope_cfg=(q_width, kv_width, kvw, hd))


def out_proj(o, w, res):
    m, d = res.shape
    return matmul(o, w, tm=_tile(m, MM_ROWS, SUBLANE), tn=_tile(d, 512), epilogue="residual", res=res)


def glu_proj(z, w_glu, res, time_major=None):
    m, d = res.shape
    rows = m if time_major is None else time_major[1]
    return matmul(z, w_glu, tm=_tile(rows, MM_ROWS, SUBLANE), tn=_tile(d, 512), epilogue="sglu", n_out=d,
                  w2_col_off=d, res=res, a_time_major=time_major)


def ffn(h, hn, w_gu, w_down):
    m, d = h.shape
    d_ff = w_down.shape[0]
    tm = _tile(m, MM_ROWS, SUBLANE)
    act = matmul(hn, w_gu, tm=tm, tn=_tile(d_ff, 512), epilogue="swiglu", out_dtype=BF16, n_out=d_ff,
                 w2_col_off=d_ff)
    return matmul(act, w_down, tm=tm, tn=_tile(d, 256), epilogue="residual", res=h, a_buffers=1)


def kernel(x_prompt, x_sample, cache_nsa, cache_win, state_s5_re, state_s5_im, page_table, norm_mix, norm_ffn,
           norm_final, att_w_in, att_w_out, cmp_pe, cmp_w1, cmp_w2, s5_lambda_re, s5_lambda_im, s5_log_dt,
           s5_b_re, s5_b_im, s5_c_re, s5_c_im, s5_d, s5_w_glu, ffn_w_gate_up, ffn_w_down):
    batch, t_len, d = x_prompt.shape
    bd, ts, _ = x_sample.shape
    _, n_pool, page, _, n_kv, hd = cache_nsa.shape
    wb = cache_win.shape[2]
    n_pages = page_table.shape[1]
    past_len = n_pages * page
    depth = norm_mix.shape[0]
    n_heads = d // hd
    gqa = n_heads // n_kv
    q_width = n_heads * hd
    kvw = n_kv * hd
    kv_width = 6 * kvw
    assert ts == 1 and hd == LANE and wb == min(WINDOW, past_len)
    mp = batch * t_len

    att_w_in_b, att_w_out_b = cast_bf16(att_w_in), cast_bf16(att_w_out)
    s5_w_glu_b = cast_bf16(s5_w_glu)
    cmp_w1_b = cast_bf16(cmp_w1)
    ffn_w_gu_b, ffn_w_down_b = cast_bf16(ffn_w_gate_up), cast_bf16(ffn_w_down)

    hp = x_prompt.reshape(mp, d)
    hs = x_sample.reshape(bd, d)
    kv_p, kv_s, win_p, win_s = [], [], [], []
    sr_p, si_p, sr_s, si_s = [], [], [], []
    pos_p = jnp.tile(jnp.arange(t_len, dtype=jnp.int32), batch)
    pos_s = jnp.full((bd,), past_len, jnp.int32)

    for i in range(depth):
        li = i // 2
        if i % 2 == 0:
            up = rmsnorm(hp, norm_mix[i], BF16)
            us = rmsnorm(hs, norm_mix[i], BF16)
            w_in, w_out, pe, w1, w2 = att_w_in_b[li], att_w_out_b[li], cmp_pe[li], cmp_w1[li], cmp_w2[li]
            cfg = dict(q_width=q_width, kv_width=kv_width, kvw=kvw, hd=hd)

            proj = in_proj(up, w_in, pos_p, **cfg)
            proj3 = proj.reshape(batch, t_len, -1)
            kv_p.append(proj3[:, :, q_width:q_width + 4 * kvw].reshape(batch, t_len, 4, n_kv, hd))
            w_keep = min(WINDOW, t_len)
            win_p.append(proj3[:, t_len - w_keep:, q_width + 4 * kvw:q_width + 6 * kvw]
                         .reshape(batch, w_keep, 2, n_kv, hd))
            n_cmp = t_len // CMP_STRIDE - 1
            ncp = -(-n_cmp // LANE) * LANE
            kc, vc = [
                compress_rows(
                    _cmp_flat(proj3[:, :, q_width + s * kvw:q_width + (s + 1) * kvw]
                              .reshape(batch, t_len, n_kv, hd), ncp).reshape(batch * n_kv * ncp, CMP_LEN * hd),
                    pe[s], w1[s], w2[s]).reshape(batch, n_kv, ncp, hd)
                for s in (0, 1)]
            gates_r = proj[:, q_width + kv_width:].reshape(mp, n_kv, 3 * gqa).transpose(1, 0, 2)
            o = nsa_prompt_attention(proj, gates_r, kc, vc, batch=batch, t_len=t_len, n_kv=n_kv, gqa=gqa,
                                     hd=hd, n_cmp=n_cmp)
            hp = out_proj(o, w_out, hp)

            proj_s = in_proj(us, w_in, pos_s, **cfg)
            rows_s = proj_s[:, q_width:q_width + kv_width].reshape(bd, 6, n_kv, hd)
            kv_s.append(rows_s[:, None, :4])
            win_s.append(jnp.concatenate([cache_win[li][:, ts:], rows_s[:, None, 4:]], axis=1))
            tk_len = past_len + ts
            assert past_len % CMP_STRIDE == 0 and ts < CMP_STRIDE
            n_cmp_s = tk_len // CMP_STRIDE - 1
            ncp_s = -(-n_cmp_s // LANE) * LANE
            cache4 = cache_nsa[li].reshape(n_pool, page, 4 * n_kv, hd)
            kvc = compress_paged(page_table, cache4, pe, cmp_w1_b[li], w2, n_kv=n_kv, hd=hd, ncp=ncp_s)
            q_s = proj_s[:, :q_width].reshape(bd, n_kv, gqa, hd)
            n_sel_s = -(-tk_len // SEL_BLOCK)
            k_sel_s = min(N_SELECT, n_sel_s)
            o_cmp, idx = nsa_dec_select(q_s, kvc[0], kvc[1], n_cmp=n_cmp_s, n_sel=n_sel_s, pos_q=past_len)
            new_rows = jnp.pad(rows_s.transpose(0, 2, 1, 3), ((0, 0), (0, 0), (0, SUBLANE - 6), (0, 0)))
            gates_s = proj_s[:, q_width + kv_width:].reshape(bd, n_kv, gqa, 3)
            o_s = nsa_dec_attend(page_table, idx[:, :, 0, :k_sel_s].reshape(-1), q_s, cache4, new_rows,
                                 cache_win[li].reshape(bd, wb, 2 * kvw), o_cmp, gates_s, k_sel=k_sel_s,
                                 past_len=past_len, pos_q=past_len)
            hs = out_proj(o_s.reshape(bd, q_width), w_out, hs)
        else:
            up = rmsnorm(hp, norm_mix[i], F32, time_major=(batch, t_len))
            us = rmsnorm(hs, norm_mix[i], F32)
            prm = s5_prepare(s5_lambda_re[li], s5_lambda_im[li], s5_log_dt[li], s5_b_re[li], s5_b_im[li],
                             s5_c_re[li], s5_c_im[li], s5_d[li])
            g_cnt, n_state = s5_lambda_re.shape[1:]
            w_glu = s5_w_glu_b[li]

            zeros = jnp.zeros((batch, g_cnt, n_state), F32)
            z, hr, hi = s5_scan(up, zeros, zeros, prm, batch=batch, t_len=t_len)
            sr_p.append(hr)
            si_p.append(hi)
            hp = glu_proj(z, w_glu, hp, time_major=(batch, t_len))

            z, hr, hi = s5_scan(us, state_s5_re[li].astype(F32), state_s5_im[li].astype(F32), prm,
                                batch=bd, t_len=ts)
            sr_s.append(hr)
            si_s.append(hi)
            hs = glu_proj(z, w_glu, hs)

        hp = ffn(hp, rmsnorm(hp, norm_ffn[i], BF16), ffn_w_gu_b[i], ffn_w_down_b[i])
        hs = ffn(hs, rmsnorm(hs, norm_ffn[i], BF16), ffn_w_gu_b[i], ffn_w_down_b[i])

    y_p = rmsnorm(hp, norm_final, F32).reshape(batch, t_len, d)
    y_s = rmsnorm(hs, norm_final, F32).reshape(bd, ts, d)
    return (y_p, y_s, jnp.stack(kv_p), jnp.stack(kv_s), jnp.stack(win_p), jnp.stack(win_s),
            jnp.stack(sr_p), jnp.stack(si_p), jnp.stack(sr_s), jnp.stack(si_s))
```

```python
import functools

import jax
import jax.numpy as jnp
from jax import lax
from jax.experimental import pallas as pl
from jax.experimental.pallas import tpu as pltpu

EPS = 1e-6
NEG_INF = -1e30
ROPE_THETA = 10000.0
CMP_STRIDE = 16
CMP_LEN = 2 * CMP_STRIDE
SEL_BLOCK = 64
N_SELECT = 16
WINDOW = 512
Q_BLK = 64

LANE = 128
SUBLANE = 8
VMEM_LIMIT_BYTES = 56 * 1024 * 1024

F32 = jnp.float32
BF16 = jnp.bfloat16
_NT = (((1,), (1,)), ((), ()))
LOG2_E = 1.4426950408889634


def _cparams(semantics):
    return pltpu.CompilerParams(dimension_semantics=semantics, vmem_limit_bytes=VMEM_LIMIT_BYTES)


def _tile(dim, pref, align=LANE):
    if dim <= pref:
        return dim
    t = (pref // align) * align
    while t >= align:
        if dim % t == 0:
            return t
        t -= align
    return dim


def _rmsnorm_kernel(x_ref, g_ref, o_ref):
    x = x_ref[...]
    y = x * lax.rsqrt(jnp.mean(x * x, axis=-1, keepdims=True) + EPS)
    o_ref[...] = (y * g_ref[...]).astype(o_ref.dtype)


def rmsnorm(x, g, out_dtype, time_major=None):
    m, d = x.shape
    if time_major is None:
        tm = _tile(m, 256, SUBLANE)
        out_spec = pl.BlockSpec((tm, d), lambda i: (i, 0))
        out_shape = (m, d)
    else:
        batch, t_len = time_major
        tm = _tile(t_len, 256, SUBLANE)
        nt = t_len // tm
        out_spec = pl.BlockSpec((tm, d), lambda i: (i % nt, i // nt))
        out_shape = (t_len, batch * d)
    out = pl.pallas_call(
        _rmsnorm_kernel,
        grid=(m // tm,),
        in_specs=[pl.BlockSpec((tm, d), lambda i: (i, 0)), pl.BlockSpec((1, d), lambda i: (0, 0))],
        out_specs=out_spec,
        out_shape=jax.ShapeDtypeStruct(out_shape, out_dtype),
        compiler_params=_cparams(("parallel",)),
        name="rmsnorm",
    )(x, g.reshape(1, d))
    return out.reshape(m, d)


def _cast_kernel(x_ref, o_ref):
    o_ref[...] = x_ref[...].astype(o_ref.dtype)


def cast_bf16(w):
    shape = w.shape
    w2 = w.reshape(-1, shape[-1])
    r, c = w2.shape
    tr = _tile(r, 256, SUBLANE)
    tc = c if c <= 8192 else _tile(c, 8192)
    out = pl.pallas_call(
        _cast_kernel,
        grid=(r // tr, c // tc),
        in_specs=[pl.BlockSpec((tr, tc), lambda i, j: (i, j))],
        out_specs=pl.BlockSpec((tr, tc), lambda i, j: (i, j)),
        out_shape=jax.ShapeDtypeStruct((r, c), BF16),
        compiler_params=_cparams(("parallel", "parallel")),
        name="cast_bf16",
    )(w2)
    return out.reshape(shape)


def _mm_kernel(*refs, tn, dual, epilogue, rope_cfg):
    it = iter(refs)
    a_ref = next(it)
    w_ref = next(it)
    w2_ref = next(it) if dual else None
    res_ref = next(it) if epilogue in ("residual", "sglu") else None
    cos_ref = next(it) if epilogue == "rope" else None
    sin_ref = next(it) if epilogue == "rope" else None
    o_ref = next(it)

    a = a_ref[...]
    acc = jnp.dot(a, w_ref[...], preferred_element_type=F32)
    if dual:
        acc2 = jnp.dot(a, w2_ref[...], preferred_element_type=F32)

    if epilogue == "residual":
        o_ref[...] = (res_ref[...] + acc).astype(o_ref.dtype)
    elif epilogue == "swiglu":
        o_ref[...] = (jax.nn.silu(acc) * acc2).astype(o_ref.dtype)
    elif epilogue == "sglu":
        o_ref[...] = (res_ref[...] + acc * jax.nn.sigmoid(acc2)).astype(o_ref.dtype)
    elif epilogue == "rope":
        q_width, kv_width, kvw, hd = rope_cfg
        col0 = pl.program_id(1) * tn
        in_kv = (col0 >= q_width) & (col0 < q_width + kv_width)
        is_rope = (col0 < q_width) | (in_kv & (((col0 - q_width) // kvw) % 2 == 0))

        @pl.when(is_rope)
        def _():
            cos = cos_ref[...]
            sin = sin_ref[...]
            for c in range(tn // hd):
                x = acc[:, c * hd:(c + 1) * hd]
                o_ref[:, c * hd:(c + 1) * hd] = x * cos + pltpu.roll(x, hd // 2, 1) * sin

        @pl.when(jnp.logical_not(is_rope))
        def _():
            o_ref[...] = acc


def matmul(a, w, layer, *, tm, tn, epilogue, out_dtype=F32, n_out=None, w2_col_off=None, res=None, cos=None,
           sin=None, rope_cfg=None, a_buffers=2, a_time_major=None):
    m, kdim = a.shape
    n = w.shape[-1] if n_out is None else n_out
    dual = epilogue in ("swiglu", "sglu")
    assert m % tm == 0 and w.ndim == 3

    def w_spec(col_off):
        return pl.BlockSpec((None, kdim, tn), lambda i, j: (layer, 0, j + col_off))
    a_spec = pl.BlockSpec((tm, kdim), lambda i, j: (i, 0))
    if a_buffers != 2:
        a_spec = pl.BlockSpec((tm, kdim), lambda i, j: (i, 0), pipeline_mode=pl.Buffered(a_buffers))
    if a_time_major is not None:
        batch, t_len = a_time_major
        assert t_len % tm == 0 and batch * t_len == m
        nt = t_len // tm
        a = a.reshape(t_len, batch * kdim)
        a_spec = pl.BlockSpec((tm, kdim), lambda i, j: (i % nt, i // nt))
    in_specs = [a_spec, w_spec(0)]
    args = [a, w]
    if dual:
        assert w2_col_off % tn == 0 and n % tn == 0
        in_specs.append(w_spec(w2_col_off // tn))
        args.append(w)
    if epilogue in ("residual", "sglu"):
        in_specs.append(pl.BlockSpec((tm, tn), lambda i, j: (i, j)))
        args.append(res)
    if epilogue == "rope":
        hd = rope_cfg[3]
        in_specs += [pl.BlockSpec((tm, hd), lambda i, j: (i, 0))] * 2
        args += [cos, sin]
    return pl.pallas_call(
        functools.partial(_mm_kernel, tn=tn, dual=dual, epilogue=epilogue, rope_cfg=rope_cfg),
        grid=(m // tm, pl.cdiv(n, tn)),
        in_specs=in_specs,
        out_specs=pl.BlockSpec((tm, tn), lambda i, j: (i, j)),
        out_shape=jax.ShapeDtypeStruct((m, n), out_dtype),
        compiler_params=_cparams(("parallel", "arbitrary")),
        name="mm_" + epilogue,
    )(*args)


def _compress_kernel(x_ref, pe_ref, w1_ref, w2_ref, o_ref, w1_s, w2_s):
    @pl.when(pl.program_id(0) == 0)
    def _():
        w1_s[...] = w1_ref[...].astype(BF16)
        w2_s[...] = w2_ref[...].astype(BF16)

    x = (x_ref[...] + pe_ref[...]).astype(BF16)
    h = jax.nn.gelu(jnp.dot(x, w1_s[...], preferred_element_type=F32))
    o_ref[...] = jnp.dot(h.astype(BF16), w2_s[...], preferred_element_type=F32).astype(o_ref.dtype)


def compress_rows(flat, pe, w1, w2):
    r, kdim = flat.shape
    hid, hd = w2.shape
    tr = _tile(r, 512, SUBLANE)
    return pl.pallas_call(
        _compress_kernel,
        grid=(r // tr,),
        in_specs=[pl.BlockSpec((tr, kdim), lambda i: (i, 0)),
                  pl.BlockSpec((1, kdim), lambda i: (0, 0)),
                  pl.BlockSpec((kdim, hid), lambda i: (0, 0)),
                  pl.BlockSpec((hid, hd), lambda i: (0, 0))],
        out_specs=pl.BlockSpec((tr, hd), lambda i: (i, 0)),
        out_shape=jax.ShapeDtypeStruct((r, hd), BF16),
        scratch_shapes=[pltpu.VMEM((kdim, hid), BF16), pltpu.VMEM((hid, hd), BF16)],
        compiler_params=_cparams(("arbitrary",)),
        name="compress",
    )(flat, pe.reshape(1, kdim), w1, w2)


def _cmp_flat(x, ncp):
    b, tk, n_kv, hd = x.shape
    n_chunk = tk // CMP_STRIDE
    c = x[:, :n_chunk * CMP_STRIDE].reshape(b, n_chunk, CMP_STRIDE, n_kv, hd)
    blk = jnp.concatenate([c[:, :-1], c[:, 1:]], axis=2)
    flat = blk.transpose(0, 3, 1, 2, 4).reshape(b, n_kv, n_chunk - 1, CMP_LEN * hd)
    return jnp.pad(flat, ((0, 0), (0, 0), (0, ncp - (n_chunk - 1)), (0, 0)))


def _overlap_matrix(ncp, nsp):
    ic = jnp.arange(ncp, dtype=jnp.int32)[:, None] * CMP_STRIDE
    js = jnp.arange(nsp, dtype=jnp.int32)[None, :] * SEL_BLOCK
    return ((ic < js + SEL_BLOCK) & (ic + CMP_LEN > js)).astype(F32)


def _compress_paged_kernel(*refs, pg, page, hd, n_kv, n_groups, n_chunk, ncp):
    pt_ref = refs[0]
    page_refs = refs[1:1 + pg]
    pe_ref, w1_ref, w2_ref, o_ref, la_s, lb_s, tail_s = refs[1 + pg:]
    del pt_ref, n_groups
    g = pl.program_id(1)
    cpp = page // CMP_STRIDE
    rows_g = pg * cpp
    half = CMP_STRIDE * hd
    tail = 2 * SUBLANE

    if ncp > n_chunk:
        @pl.when(g == 0)
        def _():
            o_ref[...] = jnp.zeros_like(o_ref)

    for i in range(pg):
        for r in range(CMP_STRIDE):
            x = page_refs[i][pl.ds(r, cpp, stride=CMP_STRIDE), :, :]
            xt = jnp.swapaxes(x, 0, 1)
            for j in range(2 * n_kv):
                s, k = divmod(j, n_kv)
                rows = slice(k * rows_g + i * cpp, k * rows_g + (i + 1) * cpp)
                la_s[s, rows, r * hd:(r + 1) * hd] = xt[j] + pe_ref[s, r:r + 1, :]
                lb_s[s, rows, r * hd:(r + 1) * hd] = xt[j] + pe_ref[s, CMP_STRIDE + r:CMP_STRIDE + r + 1, :]

    row0 = pl.multiple_of(g * rows_g, rows_g)
    last = lax.broadcasted_iota(jnp.int32, (tail, 1), 0) == tail - 1
    for s in range(2):
        ca = jnp.dot(la_s[s].astype(BF16), w1_ref[s, :half, :], preferred_element_type=F32)
        cb = jnp.dot(lb_s[s].astype(BF16), w1_ref[s, half:, :], preferred_element_type=F32)
        w2 = w2_ref[s].astype(BF16)
        for k in range(n_kv):
            j = s * n_kv + k
            ca_k = ca[k * rows_g:(k + 1) * rows_g]
            cb_k = cb[k * rows_g:(k + 1) * rows_g]

            @pl.when(g > 0)
            def _(j=j, s=s, k=k, cb_k=cb_k, w2=w2):
                h = jax.nn.gelu(tail_s[j] + cb_k[0:1, :])
                o_new = jnp.dot(h.astype(BF16), w2, preferred_element_type=F32)
                prev = pl.ds(row0 - tail, tail)
                o_old = o_ref[s, k, prev, :].astype(F32)
                o_ref[s, k, prev, :] = jnp.where(last, o_new, o_old).astype(o_ref.dtype)

            h = jax.nn.gelu(ca_k + pltpu.roll(cb_k, rows_g - 1, 0))
            o_ref[s, k, pl.ds(row0, rows_g), :] = jnp.dot(h.astype(BF16), w2,
                                                          preferred_element_type=F32).astype(o_ref.dtype)
            tail_s[j] = ca_k[rows_g - tail:, :]


def compress_paged(page_table, cache4, pe, w1, w2, *, n_kv, hd, ncp):
    bd, n_pages = page_table.shape
    page = cache4.shape[1]
    assert page % CMP_STRIDE == 0
    cpp = page // CMP_STRIDE
    n_chunk = n_pages * cpp
    assert ncp >= n_chunk and cpp == SUBLANE and 2 * n_kv == SUBLANE
    pg = _tile(n_pages, 16, 1)
    n_groups = n_pages // pg
    hid = w1.shape[-1]

    def page_spec(i):
        return pl.BlockSpec((None, page, 2 * n_kv, hd), lambda b, g, pt: (pt[b * n_pages + g * pg + i], 0, 0, 0))

    kern = functools.partial(_compress_paged_kernel, pg=pg, page=page, hd=hd, n_kv=n_kv, n_groups=n_groups,
                             n_chunk=n_chunk, ncp=ncp)
    grid_spec = pltpu.PrefetchScalarGridSpec(
        num_scalar_prefetch=1,
        grid=(bd, n_groups),
        in_specs=[page_spec(i) for i in range(pg)] + [
            pl.BlockSpec((2, CMP_LEN, hd), lambda b, g, pt: (0, 0, 0)),
            pl.BlockSpec((2, CMP_LEN * hd, hid), lambda b, g, pt: (0, 0, 0)),
            pl.BlockSpec((2, hid, hd), lambda b, g, pt: (0, 0, 0))],
        out_specs=pl.BlockSpec((2, None, n_kv, ncp, hd), lambda b, g, pt: (0, b, 0, 0, 0)),
        scratch_shapes=[pltpu.VMEM((2, n_kv * pg * cpp, CMP_STRIDE * hd), F32),
                        pltpu.VMEM((2, n_kv * pg * cpp, CMP_STRIDE * hd), F32),
                        pltpu.VMEM((2 * n_kv, 2 * SUBLANE, hid), F32)],
    )
    return pl.pallas_call(
        kern,
        grid_spec=grid_spec,
        out_shape=jax.ShapeDtypeStruct((2, bd, n_kv, ncp, hd), BF16),
        compiler_params=_cparams(("arbitrary", "arbitrary")),
        name="compress_paged",
    )(page_table.reshape(-1), *([cache4] * pg), pe, w1, w2)


def _nsa_prompt_kernel(q_ref, ks_ref, vs_ref, kw_ref, vw_ref, kc_ref, vc_ref, gate_ref, ovl_ref, blk_ref,
                       o_ref, ks_s, vs_s, kw_s, vw_s, acc_s, mx_s, *, gqa, hd, t_len, n_cmp, n_sel, k_sel,
                       sel_ext, scale):
    qb = pl.program_id(2)

    @pl.when(qb == 0)
    def _():
        ks_s[:, :hd] = ks_ref[...].astype(BF16)
        ks_s[:, hd:] = blk_ref[...]
        vs_s[:, :hd] = vs_ref[...].astype(BF16)
        vs_s[:, hd:] = jnp.ones((t_len, LANE), BF16)
        kw_s[...] = kw_ref[...].astype(BF16)
        vw_s[:, :hd] = vw_ref[...].astype(BF16)
        vw_s[:, hd:] = jnp.ones((t_len, LANE), BF16)

    rows = gqa * Q_BLK
    b0 = qb * Q_BLK
    q = q_ref[...]
    qg = jnp.concatenate([q[:, g * hd:(g + 1) * hd] for g in range(gqa)], axis=0).astype(BF16)
    pos_q = b0 + lax.broadcasted_iota(jnp.int32, (Q_BLK, 1), 0)

    c_exp = scale * LOG2_E

    def exp_shifted(s3):
        return jnp.exp2((s3 - jnp.max(s3, axis=-1, keepdims=True)) * c_exp)

    ncp = kc_ref.shape[0]
    s = lax.dot_general(qg, kc_ref[...], _NT, preferred_element_type=F32)
    ci = lax.broadcasted_iota(jnp.int32, (1, ncp), 1)
    m_cmp = ((ci * CMP_STRIDE + (CMP_LEN - 1)) <= pos_q) & (ci < n_cmp)
    s3 = s.reshape(gqa, Q_BLK, ncp) + jnp.where(m_cmp, 0.0, NEG_INF)[None]
    e3 = exp_shifted(s3) * m_cmp.astype(F32)[None]
    l3 = jnp.sum(e3, axis=-1, keepdims=True)
    p3 = e3 * jnp.where(l3 > 0.0, 1.0 / l3, 0.0)
    o_cmp = jnp.dot(p3.reshape(rows, ncp).astype(BF16), vc_ref[...], preferred_element_type=F32)

    nsp = ovl_ref.shape[1]
    imp = jnp.dot(jnp.sum(p3, axis=0), ovl_ref[...], preferred_element_type=F32,
                  precision=lax.Precision.HIGHEST)
    jj = lax.broadcasted_iota(jnp.int32, (1, nsp), 1)
    cur = pos_q // SEL_BLOCK
    forced = (jj == 0) | (jj == cur) | (jj == cur - 1)
    future = jj * SEL_BLOCK > pos_q
    score = jnp.where(future, -jnp.inf, jnp.where(forced, jnp.inf, imp))
    rank = jnp.zeros((Q_BLK, nsp), F32)
    for j2 in range(n_sel):
        col = score[:, j2:j2 + 1]
        beats = (col > score) | ((col == score) & (j2 < jj))
        rank = rank + beats.astype(F32)
    sel_ok = (rank < k_sel) & (jj < qb)
    nb = jnp.where(sel_ok, 0.0, NEG_INF).astype(BF16)
    q_aug = jnp.concatenate([qg, jnp.concatenate([nb] * gqa, axis=0)], axis=1)

    tq = lax.broadcasted_iota(jnp.int32, (Q_BLK, 1), 0)
    bias_d = jnp.where(lax.broadcasted_iota(jnp.int32, (1, SEL_BLOCK), 1) <= tq, 0.0, NEG_INF)
    diag = pl.ds(pl.multiple_of(b0, SEL_BLOCK), SEL_BLOCK)
    s_d = (lax.dot_general(qg, ks_s[diag, :hd], _NT, preferred_element_type=F32)
           + jnp.concatenate([bias_d] * gqa, axis=0))
    m_d = jnp.max(s_d, axis=-1, keepdims=True)

    def prefix(kv_len):
        s = lax.dot_general(q_aug, ks_s[:kv_len, :], _NT, preferred_element_type=F32)
        mx = jnp.maximum(jnp.max(s, axis=-1, keepdims=True), m_d)
        e = jnp.exp2((s - mx) * c_exp).astype(BF16)
        acc_s[...] = jnp.dot(e, vs_s[:kv_len, :], preferred_element_type=F32)
        mx_s[...] = mx

    def no_prefix():
        acc_s[...] = jnp.zeros_like(acc_s)
        mx_s[...] = m_d

    n_ext = (b0 + sel_ext - 1) // sel_ext
    pl.when(n_ext == 0)(no_prefix)
    for c in range(1, t_len // sel_ext + 1):
        pl.when(n_ext == c)(functools.partial(prefix, c * sel_ext))
    e_d = jnp.exp2((s_d - mx_s[...]) * c_exp).astype(BF16)
    acc = acc_s[...] + jnp.dot(e_d, vs_s[diag, :], preferred_element_type=F32)
    o_sel = acc[:, :hd] / acc[:, hd:hd + 1]

    wk = WINDOW + Q_BLK
    start = pl.multiple_of(jnp.maximum(b0 - WINDOW, 0), Q_BLK)
    s = lax.dot_general(qg, kw_s[pl.ds(start, wk), :], _NT, preferred_element_type=F32)
    dist = pos_q - (start + lax.broadcasted_iota(jnp.int32, (1, wk), 1))
    m_win = (dist >= 0) & (dist <= WINDOW)
    s3 = s.reshape(gqa, Q_BLK, wk) + jnp.where(m_win, 0.0, NEG_INF)[None]
    e3 = exp_shifted(s3)
    acc_w = jnp.dot(e3.reshape(rows, wk).astype(BF16), vw_s[pl.ds(start, wk), :], preferred_element_type=F32)
    o_win = acc_w[:, :hd] / acc_w[:, hd:hd + 1]

    gate = jax.nn.sigmoid(gate_ref[...])
    for g in range(gqa):
        r = slice(g * Q_BLK, (g + 1) * Q_BLK)
        o = (gate[:, 3 * g:3 * g + 1] * o_cmp[r] + gate[:, 3 * g + 1:3 * g + 2] * o_sel[r]
             + gate[:, 3 * g + 2:3 * g + 3] * o_win[r])
        o_ref[:, g * hd:(g + 1) * hd] = o.astype(o_ref.dtype)


def nsa_prompt_attention(proj, gates_r, kc, vc, *, batch, t_len, n_kv, gqa, hd, n_cmp):
    m = batch * t_len
    nq = t_len // Q_BLK
    q_width = n_kv * gqa * hd
    kvw = n_kv * hd
    ncp = kc.shape[2]
    n_sel = -(-t_len // SEL_BLOCK)
    nsp = -(-n_sel // LANE) * LANE
    k_sel = min(N_SELECT, n_sel)
    assert t_len % Q_BLK == 0 and t_len >= WINDOW + Q_BLK and Q_BLK == SEL_BLOCK
    ovl = _overlap_matrix(ncp, nsp)
    blk_onehot = (jnp.arange(t_len, dtype=jnp.int32)[:, None] // SEL_BLOCK
                  == jnp.arange(nsp, dtype=jnp.int32)[None, :]).astype(BF16)

    def kv_spec(slot):
        base = (q_width + slot * kvw) // hd
        return pl.BlockSpec((t_len, hd), lambda b, k, i: (b, base + k))

    sel_ext = _tile(t_len, max(t_len // 4, LANE))
    kern = functools.partial(_nsa_prompt_kernel, gqa=gqa, hd=hd, t_len=t_len, n_cmp=n_cmp, n_sel=n_sel,
                             k_sel=k_sel, sel_ext=sel_ext, scale=hd ** -0.5)
    return pl.pallas_call(
        kern,
        grid=(batch, n_kv, nq),
        in_specs=[pl.BlockSpec((Q_BLK, gqa * hd), lambda b, k, i: (b * nq + i, k)),
                  kv_spec(2), kv_spec(3), kv_spec(4), kv_spec(5),
                  pl.BlockSpec((None, None, ncp, hd), lambda b, k, i: (b, k, 0, 0)),
                  pl.BlockSpec((None, None, ncp, hd), lambda b, k, i: (b, k, 0, 0)),
                  pl.BlockSpec((None, Q_BLK, 3 * gqa), lambda b, k, i: (k, b * nq + i, 0)),
                  pl.BlockSpec((ncp, nsp), lambda b, k, i: (0, 0)),
                  pl.BlockSpec((t_len, nsp), lambda b, k, i: (0, 0))],
        out_specs=pl.BlockSpec((Q_BLK, gqa * hd), lambda b, k, i: (b * nq + i, k)),
        out_shape=jax.ShapeDtypeStruct((m, q_width), BF16),
        scratch_shapes=[pltpu.VMEM((t_len, hd + nsp), BF16), pltpu.VMEM((t_len, hd + LANE), BF16),
                        pltpu.VMEM((t_len, hd), BF16), pltpu.VMEM((t_len, hd + LANE), BF16),
                        pltpu.VMEM((gqa * Q_BLK, hd + LANE), F32), pltpu.VMEM((gqa * Q_BLK, 1), F32)],
        compiler_params=_cparams(("parallel", "parallel", "arbitrary")),
        name="nsa_prompt",
    )(proj, proj, proj, proj, proj, kc, vc, gates_r, ovl, blk_onehot)


def _nsa_dec_select_kernel(q_ref, kc_ref, vc_ref, ovl_ref, ocmp_ref, idx_ref, *, n_cmp, n_sel, pos_q, scale):
    q = q_ref[...].astype(BF16)
    ncp = kc_ref.shape[0]
    nsp = ovl_ref.shape[1]
    s = lax.dot_general(q, kc_ref[...], _NT, preferred_element_type=F32) * scale
    ci = lax.broadcasted_iota(jnp.int32, (1, ncp), 1)
    m_cmp = ((ci * CMP_STRIDE + (CMP_LEN - 1)) <= pos_q) & (ci < n_cmp)
    s = jnp.where(m_cmp, s, NEG_INF)
    e = jnp.where(m_cmp, jnp.exp(s - jnp.max(s, axis=-1, keepdims=True)), 0.0)
    l = jnp.sum(e, axis=-1, keepdims=True)
    p = e * jnp.where(l > 0.0, 1.0 / l, 0.0)
    ocmp_ref[...] = jnp.dot(p.astype(BF16), vc_ref[...], preferred_element_type=F32)

    imp = jnp.sum(jnp.dot(p, ovl_ref[...], preferred_element_type=F32, precision=lax.Precision.HIGHEST),
                  axis=0, keepdims=True)
    jj = lax.broadcasted_iota(jnp.int32, (1, nsp), 1)
    cur = pos_q // SEL_BLOCK
    forced = (jj == 0) | (jj == cur) | (jj == cur - 1)
    future = jj * SEL_BLOCK > pos_q
    score = jnp.where(future, -jnp.inf, jnp.where(forced, jnp.inf, imp))
    rr = lax.broadcasted_iota(jnp.int32, (nsp, nsp), 0)
    cc = lax.broadcasted_iota(jnp.int32, (nsp, nsp), 1)
    score_b = jnp.broadcast_to(score, (nsp, nsp))
    score_col = jnp.sum(jnp.where(rr == cc, score_b, 0.0), axis=1, keepdims=True)
    beaten = ((score_b > score_col) | ((score_b == score_col) & (cc < rr))) & (cc < n_sel)
    rank_col = jnp.sum(beaten.astype(F32), axis=1, keepdims=True)
    lane = lax.broadcasted_iota(jnp.int32, (nsp, LANE), 1)
    row = lax.broadcasted_iota(jnp.int32, (nsp, LANE), 0)
    hit = (rank_col == lane.astype(F32)) & (row < n_sel)
    idx_ref[...] = jnp.sum(jnp.where(hit, row.astype(F32), 0.0), axis=0, keepdims=True).astype(jnp.int32)


def nsa_dec_select(q, kc, vc, *, n_cmp, n_sel, pos_q):
    bd, n_kv, gqa, hd = q.shape
    ncp = kc.shape[2]
    nsp = -(-n_sel // LANE) * LANE
    kern = functools.partial(_nsa_dec_select_kernel, n_cmp=n_cmp, n_sel=n_sel, pos_q=pos_q, scale=hd ** -0.5)
    return pl.pallas_call(
        kern,
        grid=(bd, n_kv),
        in_specs=[pl.BlockSpec((None, None, gqa, hd), lambda b, k: (b, k, 0, 0)),
                  pl.BlockSpec((None, None, ncp, hd), lambda b, k: (b, k, 0, 0)),
                  pl.BlockSpec((None, None, ncp, hd), lambda b, k: (b, k, 0, 0)),
                  pl.BlockSpec((ncp, nsp), lambda b, k: (0, 0))],
        out_specs=[pl.BlockSpec((None, None, gqa, hd), lambda b, k: (b, k, 0, 0)),
                   pl.BlockSpec((None, None, 1, LANE), lambda b, k: (b, k, 0, 0))],
        out_shape=[jax.ShapeDtypeStruct((bd, n_kv, gqa, hd), F32),
                   jax.ShapeDtypeStruct((bd, n_kv, 1, LANE), jnp.int32)],
        compiler_params=_cparams(("parallel", "parallel")),
        name="nsa_dec_select",
    )(q, kc, vc, _overlap_matrix(ncp, nsp))


def _nsa_dec_attend_kernel(*refs, k_sel, n_kv, past_len, pos_q, scale):
    pt_ref, idx_ref, q_ref = refs[:3]
    kv_refs = refs[3:3 + k_sel]
    new_ref, kw_ref, vw_ref, ocmp_ref, gate_ref, o_ref = refs[3 + k_sel:]
    del pt_ref
    b = pl.program_id(0)
    k = pl.program_id(1)

    q = q_ref[...].astype(BF16)
    new = new_ref[...].astype(BF16)
    newf = new.astype(F32)
    s_all = lax.dot_general(q, new, _NT, preferred_element_type=F32) * scale

    first = (lax.broadcasted_iota(jnp.int32, (SEL_BLOCK, 1), 0) == 0).astype(F32)
    new_k = first * new_ref[2:3, :]
    new_v = first * new_ref[3:4, :]
    off = lax.broadcasted_iota(jnp.int32, (1, SEL_BLOCK), 1)
    s_blk, v_blk = [], []
    for r in range(k_sel):
        tok0 = idx_ref[(b * n_kv + k) * k_sel + r] * SEL_BLOCK
        is_new = tok0 >= past_len
        kb = jnp.where(is_new, new_k, kv_refs[r][:, k, :]).astype(BF16)
        v_blk.append(jnp.where(is_new, new_v, kv_refs[r][:, n_kv + k, :]).astype(BF16))
        s = lax.dot_general(q, kb, _NT, preferred_element_type=F32) * scale
        s_blk.append(jnp.where(tok0 + off <= pos_q, s, NEG_INF))
    mx = functools.reduce(jnp.maximum, [jnp.max(s, axis=-1, keepdims=True) for s in s_blk])
    l_sel = jnp.zeros_like(mx)
    acc = jnp.zeros((q.shape[0], q.shape[1]), F32)
    for s, vb in zip(s_blk, v_blk):
        e = jnp.exp(s - mx)
        l_sel = l_sel + jnp.sum(e, axis=-1, keepdims=True)
        acc = acc + jnp.dot(e.astype(BF16), vb, preferred_element_type=F32)
    o_sel = acc / l_sel

    wb = kw_ref.shape[0]
    s = lax.dot_general(q, kw_ref[...].astype(BF16), _NT, preferred_element_type=F32) * scale
    pos_w = (past_len - wb) + lax.broadcasted_iota(jnp.int32, (1, wb), 1)
    dist = pos_q - pos_w
    m_win = (dist >= 0) & (dist <= WINDOW) & (pos_w >= 0)
    s = jnp.where(m_win, s, NEG_INF)
    s_nw = s_all[:, 4:5]
    mx = jnp.maximum(jnp.max(s, axis=-1, keepdims=True), s_nw)
    e = jnp.where(m_win, jnp.exp(s - mx), 0.0)
    e_nw = jnp.exp(s_nw - mx)
    l_w = jnp.sum(e, axis=-1, keepdims=True) + e_nw
    o_win = (jnp.dot(e.astype(BF16), vw_ref[...].astype(BF16), preferred_element_type=F32)
             + e_nw.astype(BF16).astype(F32) * newf[5:6, :]) / l_w

    gate = jax.nn.sigmoid(gate_ref[...])
    o = gate[:, 0:1] * ocmp_ref[...] + gate[:, 1:2] * o_sel + gate[:, 2:3] * o_win
    o_ref[...] = o.astype(o_ref.dtype)


def nsa_dec_attend(page_table, idx, q, cache4, new_rows, win2d, o_cmp, gates, *, k_sel, past_len, pos_q):
    bd, n_kv, gqa, hd = q.shape
    n_pages = page_table.shape[1]
    page = cache4.shape[1]
    wb = win2d.shape[1]
    assert page % SEL_BLOCK == 0 and past_len % SEL_BLOCK == 0 and pos_q == past_len
    assert 2 * n_kv == SUBLANE and cache4.shape[2] == 4 * n_kv
    bpp = page // SEL_BLOCK
    kern = functools.partial(_nsa_dec_attend_kernel, k_sel=k_sel, n_kv=n_kv, past_len=past_len, pos_q=pos_q,
                             scale=hd ** -0.5)

    def blk_spec(r):
        def index_map(b, k, pt, ix):
            j = ix[(b * n_kv + k) * k_sel + r]
            return pt[b * n_pages + jnp.minimum(j // bpp, n_pages - 1)], j % bpp, 1, 0

        return pl.BlockSpec((None, SEL_BLOCK, 2 * n_kv, hd), index_map)

    grid_spec = pltpu.PrefetchScalarGridSpec(
        num_scalar_prefetch=2,
        grid=(bd, n_kv),
        in_specs=[pl.BlockSpec((None, None, gqa, hd), lambda b, k, pt, ix: (b, k, 0, 0))]
        + [blk_spec(r) for r in range(k_sel)]
        + [pl.BlockSpec((None, None, SUBLANE, hd), lambda b, k, pt, ix: (b, k, 0, 0)),
           pl.BlockSpec((None, wb, hd), lambda b, k, pt, ix: (b, 0, k)),
           pl.BlockSpec((None, wb, hd), lambda b, k, pt, ix: (b, 0, n_kv + k)),
           pl.BlockSpec((None, None, gqa, hd), lambda b, k, pt, ix: (b, k, 0, 0)),
           pl.BlockSpec((None, None, gqa, 3), lambda b, k, pt, ix: (b, k, 0, 0))],
        out_specs=pl.BlockSpec((None, None, gqa, hd), lambda b, k, pt, ix: (b, k, 0, 0)),
    )
    return pl.pallas_call(
        kern,
        grid_spec=grid_spec,
        out_shape=jax.ShapeDtypeStruct((bd, n_kv, gqa, hd), BF16),
        compiler_params=_cparams(("arbitrary", "arbitrary")),
        name="nsa_dec_attend",
    )(page_table.reshape(-1), idx, q, *([cache4] * k_sel), new_rows, win2d, win2d, o_cmp, gates)


N_POW = SUBLANE


def _s5_disc_kernel(lre_ref, lim_ref, ldt_ref, bre_ref, bim_ref, bbre_ref, bbim_ref, pre_ref, pim_ref):
    lam_re = lre_ref[...]
    lam_im = lim_ref[...]
    dt = jnp.exp(ldt_ref[...])
    mag = jnp.exp(lam_re * dt)
    ab_re = mag * jnp.cos(lam_im * dt)
    ab_im = mag * jnp.sin(lam_im * dt)
    nr, ni = ab_re - 1.0, ab_im
    den = lam_re * lam_re + lam_im * lam_im
    f_re = (nr * lam_re + ni * lam_im) / den
    f_im = (ni * lam_re - nr * lam_im) / den
    b_re = bre_ref[...]
    b_im = bim_ref[...]
    bbre_ref[...] = f_re * b_re - f_im * b_im
    bbim_ref[...] = f_re * b_im + f_im * b_re
    pr, pi = ab_re, ab_im
    pre_ref[0] = pr
    pim_ref[0] = pi
    for n in range(1, N_POW):
        pr, pi = pr * ab_re - pi * ab_im, pr * ab_im + pi * ab_re
        pre_ref[n] = pr
        pim_ref[n] = pi


def s5_discretize(lam_re, lam_im, log_dt, b_re, b_im):
    g, n, c = b_re.shape
    rep = lambda x: jnp.broadcast_to(x[:, None, :], (g, c, n)).reshape(g * c, n)
    ldt = jnp.broadcast_to(log_dt[:, None, None], (g, c, n)).reshape(g * c, n)
    bt = lambda x: x.transpose(0, 2, 1).reshape(g * c, n)
    full = pl.BlockSpec((g * c, n), lambda: (0, 0))
    pw = pl.BlockSpec((N_POW, g * c, n), lambda: (0, 0, 0))
    bbre, bbim, pre, pim = pl.pallas_call(
        _s5_disc_kernel,
        in_specs=[full] * 5,
        out_specs=[full, full, pw, pw],
        out_shape=[jax.ShapeDtypeStruct((g * c, n), F32)] * 2
        + [jax.ShapeDtypeStruct((N_POW, g * c, n), F32)] * 2,
        compiler_params=pltpu.CompilerParams(vmem_limit_bytes=VMEM_LIMIT_BYTES),
        name="s5_discretize",
    )(rep(lam_re), rep(lam_im), ldt, bt(b_re), bt(b_im))
    pick = lambda x: x.reshape(N_POW, g, c, n)[:, :, 0, :]
    return bbre.reshape(g, c, n), bbim.reshape(g, c, n), pick(pre), pick(pim)


def _s5_scan_kernel(u_ref, h0r_ref, h0i_ref, bbr_ref, bbi_ref, ccr_ref, cci_ref, d_ref, mr_ref, mi_ref,
                    pr_ref, pi_ref, z_ref, hr_ref, hi_ref, cr_s, ci_s, *, batch, n_groups, n_sub, n_t):
    t = pl.program_id(1)
    shifts = [batch << n for n in range(8) if (batch << n) < SUBLANE]
    row = lax.broadcasted_iota(jnp.int32, (SUBLANE, 1), 0)

    @pl.when(t == 0)
    def _():
        cr_s[...] = h0r_ref[...]
        ci_s[...] = h0i_ref[...]

    bbr = bbr_ref[...].astype(BF16)
    bbi = bbi_ref[...].astype(BF16)
    ccr = ccr_ref[...].astype(BF16)
    cci = cci_ref[...].astype(BF16)
    pr = pr_ref[...]
    pi = pi_ref[...]
    cr, ci = cr_s[...], ci_s[...]

    rows_c = n_groups * SUBLANE // n_sub
    for c in range(n_sub):
        rs = slice(c * rows_c, (c + 1) * rows_c)
        u = u_ref[rs, :]
        ub = u.astype(BF16)
        xr_all = jnp.dot(ub, bbr, preferred_element_type=F32)
        xi_all = jnp.dot(ub, bbi, preferred_element_type=F32)
        hr_parts, hi_parts = [], []
        for r in range(rows_c // SUBLANE):
            xr = xr_all[r * SUBLANE:(r + 1) * SUBLANE]
            xi = xi_all[r * SUBLANE:(r + 1) * SUBLANE]
            for n, sh in enumerate(shifts):
                mr = mr_ref[n]
                mi = mi_ref[n]
                sr = pltpu.roll(xr, sh, 0)
                si = pltpu.roll(xi, sh, 0)
                xr, xi = xr + (mr * sr - mi * si), xi + (mr * si + mi * sr)
            xr, xi = xr + (pr * cr - pi * ci), xi + (pr * ci + pi * cr)
            hr_parts.append(xr)
            hi_parts.append(xi)
            cr, ci = xr, xi
            for sh in shifts:
                keep = row >= SUBLANE - sh
                cr = jnp.where(keep, cr, pltpu.roll(cr, SUBLANE - sh, 0))
                ci = jnp.where(keep, ci, pltpu.roll(ci, SUBLANE - sh, 0))
        h_re = jnp.concatenate(hr_parts, axis=0).astype(BF16)
        h_im = jnp.concatenate(hi_parts, axis=0).astype(BF16)
        y = (jnp.dot(h_re, ccr, preferred_element_type=F32) - jnp.dot(h_im, cci, preferred_element_type=F32)
             + d_ref[...] * u)
        z_ref[rs, :] = jax.nn.gelu(y).astype(z_ref.dtype)
    cr_s[...] = cr
    ci_s[...] = ci

    @pl.when(t == n_t - 1)
    def _():
        hr_ref[...] = cr
        hi_ref[...] = ci


def s5_scan(u_tb, h0_re, h0_im, prm, *, batch, t_len):
    bb_re, bb_im, cc_re, cc_im, d_skip, pw_re, pw_im = prm
    n_slab, cw, sw = bb_re.shape
    assert SUBLANE % batch == 0
    tpg = SUBLANE // batch
    assert t_len % tpg == 0
    tt = _tile(t_len, 128, tpg)
    n_t = t_len // tt
    n_groups = tt // tpg
    shifts = [batch << n for n in range(8) if (batch << n) < SUBLANE]
    row = jnp.arange(SUBLANE)[None, :, None]

    def shift_tab(pw):
        tabs = [jnp.where(row >= sh, pw[sh // batch - 1][:, None, :], 0.0) for sh in shifts]
        return jnp.stack(tabs, axis=1) if tabs else jnp.zeros((n_slab, 1, SUBLANE, sw), F32)

    carry_tab = lambda pw: pw[jnp.arange(SUBLANE) // batch].transpose(1, 0, 2)
    tile_rows = lambda h: jnp.tile(h.reshape(batch, n_slab, sw).transpose(1, 0, 2), (1, tpg, 1))
    n_tab = max(len(shifts), 1)

    slab3 = lambda r, c: pl.BlockSpec((None, r, c), lambda s, t: (s, 0, 0))
    tab4 = pl.BlockSpec((None, n_tab, SUBLANE, sw), lambda s, t: (s, 0, 0, 0))
    n_sub = max(1, n_groups * SUBLANE // LANE)
    assert n_groups % n_sub == 0
    kern = functools.partial(_s5_scan_kernel, batch=batch, n_groups=n_groups, n_sub=n_sub, n_t=n_t)
    z, hr, hi = pl.pallas_call(
        kern,
        grid=(n_slab, n_t),
        in_specs=[pl.BlockSpec((tt * batch, cw), lambda s, t: (t, s)),
                  slab3(SUBLANE, sw), slab3(SUBLANE, sw),
                  slab3(cw, sw), slab3(cw, sw), slab3(sw, cw), slab3(sw, cw),
                  pl.BlockSpec((1, cw), lambda s, t: (0, s)),
                  tab4, tab4,
                  slab3(SUBLANE, sw), slab3(SUBLANE, sw)],
        out_specs=[pl.BlockSpec((tt * batch, cw), lambda s, t: (t, s)), slab3(SUBLANE, sw), slab3(SUBLANE, sw)],
        out_shape=[jax.ShapeDtypeStruct((t_len * batch, n_slab * cw), BF16),
                   jax.ShapeDtypeStruct((n_slab, SUBLANE, sw), F32),
                   jax.ShapeDtypeStruct((n_slab, SUBLANE, sw), F32)],
        scratch_shapes=[pltpu.VMEM((SUBLANE, sw), F32), pltpu.VMEM((SUBLANE, sw), F32)],
        compiler_params=_cparams(("parallel", "arbitrary")),
        name="s5_scan",
    )(u_tb, tile_rows(h0_re), tile_rows(h0_im), bb_re, bb_im, cc_re, cc_im, d_skip.reshape(1, -1),
      shift_tab(pw_re), shift_tab(pw_im), carry_tab(pw_re), carry_tab(pw_im))
    final = lambda h: h[:, :batch].transpose(1, 0, 2).reshape(h0_re.shape)
    return z, final(hr), final(hi)


def s5_prepare(lam_re, lam_im, log_dt, b_re, b_im, c_re, c_im, d_skip):
    g, n, c = b_re.shape
    gps = LANE // c
    n_slab = g // gps
    sw = gps * n
    bbt_re, bbt_im, pw_re, pw_im = s5_discretize(lam_re, lam_im, log_dt, b_re, b_im)
    eye = jnp.eye(gps, dtype=F32)

    def blockdiag_in(bt):
        x = bt.reshape(n_slab, gps, c, 1, n) * eye[None, :, None, :, None]
        return x.reshape(n_slab, gps * c, sw)

    def blockdiag_out(cm):
        x = cm.transpose(0, 2, 1).reshape(n_slab, gps, n, 1, c) * eye[None, :, None, :, None]
        return x.reshape(n_slab, sw, gps * c)

    return (blockdiag_in(bbt_re), blockdiag_in(bbt_im), blockdiag_out(c_re.astype(F32)),
            blockdiag_out(c_im.astype(F32)), d_skip.astype(F32), pw_re.reshape(N_POW, n_slab, sw),
            pw_im.reshape(N_POW, n_slab, sw))


MM_ROWS = 1024


def _rope_tables(pos, hd):
    half = hd // 2
    inv = ROPE_THETA ** (-jnp.arange(half, dtype=F32) / half)
    ang = pos.astype(F32)[:, None] * inv[None, :]
    cos = jnp.cos(ang)
    sin = jnp.sin(ang)
    return jnp.concatenate([cos, cos], axis=-1), jnp.concatenate([-sin, sin], axis=-1)


def in_proj(u, w_in, layer, pos, *, q_width, kv_width, kvw, hd):
    m = u.shape[0]
    cos, sin = _rope_tables(pos, hd)
    return matmul(u, w_in, layer, tm=_tile(m, MM_ROWS, SUBLANE), tn=kvw, epilogue="rope", cos=cos, sin=sin,
                  rope_cfg=(q_width, kv_width, kvw, hd))


def out_proj(o, w, layer, res):
    m, d = res.shape
    return matmul(o, w, layer, tm=_tile(m, MM_ROWS, SUBLANE), tn=_tile(d, 512), epilogue="residual", res=res)


def glu_proj(z, w_glu, layer, res, time_major=None):
    m, d = res.shape
    rows = m if time_major is None else time_major[1]
    return matmul(z, w_glu, layer, tm=_tile(rows, MM_ROWS, SUBLANE), tn=_tile(d, 512), epilogue="sglu",
                  n_out=d, w2_col_off=d, res=res, a_time_major=time_major)


def ffn(h, hn, w_gu, w_down, layer):
    m, d = h.shape
    d_ff = w_down.shape[1]
    tm = _tile(m, MM_ROWS, SUBLANE)
    act = matmul(hn, w_gu, layer, tm=tm, tn=_tile(d_ff, 512), epilogue="swiglu", out_dtype=BF16, n_out=d_ff,
                 w2_col_off=d_ff)
    return matmul(act, w_down, layer, tm=tm, tn=_tile(d, 256), epilogue="residual", res=h, a_buffers=1)


def kernel(x_prompt, x_sample, cache_nsa, cache_win, state_s5_re, state_s5_im, page_table, norm_mix, norm_ffn,
           norm_final, att_w_in, att_w_out, cmp_pe, cmp_w1, cmp_w2, s5_lambda_re, s5_lambda_im, s5_log_dt,
           s5_b_re, s5_b_im, s5_c_re, s5_c_im, s5_d, s5_w_glu, ffn_w_gate_up, ffn_w_down):
    batch, t_len, d = x_prompt.shape
    bd, ts, _ = x_sample.shape
    _, n_pool, page, _, n_kv, hd = cache_nsa.shape
    wb = cache_win.shape[2]
    n_pages = page_table.shape[1]
    past_len = n_pages * page
    depth = norm_mix.shape[0]
    n_heads = d // hd
    gqa = n_heads // n_kv
    q_width = n_heads * hd
    kvw = n_kv * hd
    kv_width = 6 * kvw
    assert ts == 1 and hd == LANE and wb == min(WINDOW, past_len)
    mp = batch * t_len

    att_w_in_b, att_w_out_b = cast_bf16(att_w_in), cast_bf16(att_w_out)
    s5_w_glu_b = cast_bf16(s5_w_glu)
    cmp_w1_b = cast_bf16(cmp_w1)
    ffn_w_gu_b, ffn_w_down_b = cast_bf16(ffn_w_gate_up), cast_bf16(ffn_w_down)

    hp = x_prompt.reshape(mp, d)
    hs = x_sample.reshape(bd, d)
    kv_p, kv_s, win_p, win_s = [], [], [], []
    sr_p, si_p, sr_s, si_s = [], [], [], []
    pos_p = jnp.tile(jnp.arange(t_len, dtype=jnp.int32), batch)
    pos_s = jnp.full((bd,), past_len, jnp.int32)

    for i in range(depth):
        li = i // 2
        if i % 2 == 0:
            up = rmsnorm(hp, norm_mix[i], BF16)
            us = rmsnorm(hs, norm_mix[i], BF16)
            pe, w1, w2 = cmp_pe[li], cmp_w1[li], cmp_w2[li]
            cfg = dict(q_width=q_width, kv_width=kv_width, kvw=kvw, hd=hd)

            proj = in_proj(up, att_w_in_b, li, pos_p, **cfg)
            proj3 = proj.reshape(batch, t_len, -1)
            kv_p.append(proj3[:, :, q_width:q_width + 4 * kvw].reshape(batch, t_len, 4, n_kv, hd))
            w_keep = min(WINDOW, t_len)
            win_p.append(proj3[:, t_len - w_keep:, q_width + 4 * kvw:q_width + 6 * kvw]
                         .reshape(batch, w_keep, 2, n_kv, hd))
            n_cmp = t_len // CMP_STRIDE - 1
            ncp = -(-n_cmp // LANE) * LANE
            kc, vc = [
                compress_rows(
                    _cmp_flat(proj3[:, :, q_width + s * kvw:q_width + (s + 1) * kvw]
                              .reshape(batch, t_len, n_kv, hd), ncp).reshape(batch * n_kv * ncp, CMP_LEN * hd),
                    pe[s], w1[s], w2[s]).reshape(batch, n_kv, ncp, hd)
                for s in (0, 1)]
            gates_r = proj[:, q_width + kv_width:].reshape(mp, n_kv, 3 * gqa).transpose(1, 0, 2)
            o = nsa_prompt_attention(proj, gates_r, kc, vc, batch=batch, t_len=t_len, n_kv=n_kv, gqa=gqa,
                                     hd=hd, n_cmp=n_cmp)
            hp = out_proj(o, att_w_out_b, li, hp)

            proj_s = in_proj(us, att_w_in_b, li, pos_s, **cfg)
            rows_s = proj_s[:, q_width:q_width + kv_width].reshape(bd, 6, n_kv, hd)
            kv_s.append(rows_s[:, None, :4])
            win_s.append(jnp.concatenate([cache_win[li][:, ts:], rows_s[:, None, 4:]], axis=1))
            tk_len = past_len + ts
            assert past_len % CMP_STRIDE == 0 and ts < CMP_STRIDE
            n_cmp_s = tk_len // CMP_STRIDE - 1
            ncp_s = -(-n_cmp_s // LANE) * LANE
            cache4 = cache_nsa[li].reshape(n_pool, page, 4 * n_kv, hd)
            kvc = compress_paged(page_table, cache4, pe, cmp_w1_b[li], w2, n_kv=n_kv, hd=hd, ncp=ncp_s)
            q_s = proj_s[:, :q_width].reshape(bd, n_kv, gqa, hd)
            n_sel_s = -(-tk_len // SEL_BLOCK)
            k_sel_s = min(N_SELECT, n_sel_s)
            o_cmp, idx = nsa_dec_select(q_s, kvc[0], kvc[1], n_cmp=n_cmp_s, n_sel=n_sel_s, pos_q=past_len)
            new_rows = jnp.pad(rows_s.transpose(0, 2, 1, 3), ((0, 0), (0, 0), (0, SUBLANE - 6), (0, 0)))
            gates_s = proj_s[:, q_width + kv_width:].reshape(bd, n_kv, gqa, 3)
            o_s = nsa_dec_attend(page_table, idx[:, :, 0, :k_sel_s].reshape(-1), q_s, cache4, new_rows,
                                 cache_win[li].reshape(bd, wb, 2 * kvw), o_cmp, gates_s, k_sel=k_sel_s,
                                 past_len=past_len, pos_q=past_len)
            hs = out_proj(o_s.reshape(bd, q_width), att_w_out_b, li, hs)
        else:
            up = rmsnorm(hp, norm_mix[i], F32, time_major=(batch, t_len))
            us = rmsnorm(hs, norm_mix[i], F32)
            prm = s5_prepare(s5_lambda_re[li], s5_lambda_im[li], s5_log_dt[li], s5_b_re[li], s5_b_im[li],
                             s5_c_re[li], s5_c_im[li], s5_d[li])
            g_cnt, n_state = s5_lambda_re.shape[1:]

            zeros = jnp.zeros((batch, g_cnt, n_state), F32)
            z, hr, hi = s5_scan(up, zeros, zeros, prm, batch=batch, t_len=t_len)
            sr_p.append(hr)
            si_p.append(hi)
            hp = glu_proj(z, s5_w_glu_b, li, hp, time_major=(batch, t_len))

            z, hr, hi = s5_scan(us, state_s5_re[li].astype(F32), state_s5_im[li].astype(F32), prm,
                                batch=bd, t_len=ts)
            sr_s.append(hr)
            si_s.append(hi)
            hs = glu_proj(z, s5_w_glu_b, li, hs)

        hp = ffn(hp, rmsnorm(hp, norm_ffn[i], BF16), ffn_w_gu_b, ffn_w_down_b, i)
        hs = ffn(hs, rmsnorm(hs, norm_ffn[i], BF16), ffn_w_gu_b, ffn_w_down_b, i)

    y_p = rmsnorm(hp, norm_final, F32).reshape(batch, t_len, d)
    y_s = rmsnorm(hs, norm_final, F32).reshape(bd, ts, d)
    return (y_p, y_s, jnp.stack(kv_p), jnp.stack(kv_s), jnp.stack(win_p), jnp.stack(win_s),
            jnp.stack(sr_p), jnp.stack(si_p), jnp.stack(sr_s), jnp.stack(si_s))
```

```python
import functools

import jax
import jax.numpy as jnp
from jax import lax
from jax.experimental import pallas as pl
from jax.experimental.pallas import tpu as pltpu

EPS = 1e-6
NEG_INF = -1e30
ROPE_THETA = 10000.0
CMP_STRIDE = 16
CMP_LEN = 2 * CMP_STRIDE
SEL_BLOCK = 64
N_SELECT = 16
WINDOW = 512
Q_BLK = 64

LANE = 128
SUBLANE = 8
VMEM_LIMIT_BYTES = 56 * 1024 * 1024

F32 = jnp.float32
BF16 = jnp.bfloat16
_NT = (((1,), (1,)), ((), ()))
LOG2_E = 1.4426950408889634


def _cparams(semantics):
    return pltpu.CompilerParams(dimension_semantics=semantics, vmem_limit_bytes=VMEM_LIMIT_BYTES)


def _tile(dim, pref, align=LANE):
    if dim <= pref:
        return dim
    t = (pref // align) * align
    while t >= align:
        if dim % t == 0:
            return t
        t -= align
    return dim


def _rmsnorm_kernel(x_ref, g_ref, o_ref):
    x = x_ref[...]
    y = x * lax.rsqrt(jnp.mean(x * x, axis=-1, keepdims=True) + EPS)
    o_ref[...] = (y * g_ref[...]).astype(o_ref.dtype)


def rmsnorm(x, g, out_dtype, time_major=None):
    m, d = x.shape
    if time_major is None:
        tm = _tile(m, 256, SUBLANE)
        out_spec = pl.BlockSpec((tm, d), lambda i: (i, 0))
        out_shape = (m, d)
    else:
        batch, t_len = time_major
        tm = _tile(t_len, 256, SUBLANE)
        nt = t_len // tm
        out_spec = pl.BlockSpec((tm, d), lambda i: (i % nt, i // nt))
        out_shape = (t_len, batch * d)
    out = pl.pallas_call(
        _rmsnorm_kernel,
        grid=(m // tm,),
        in_specs=[pl.BlockSpec((tm, d), lambda i: (i, 0)), pl.BlockSpec((1, d), lambda i: (0, 0))],
        out_specs=out_spec,
        out_shape=jax.ShapeDtypeStruct(out_shape, out_dtype),
        compiler_params=_cparams(("parallel",)),
        name="rmsnorm",
    )(x, g.reshape(1, d))
    return out.reshape(m, d)


def _cast_kernel(x_ref, o_ref):
    o_ref[...] = x_ref[...].astype(o_ref.dtype)


def cast_bf16(w):
    shape = w.shape
    w2 = w.reshape(-1, shape[-1])
    r, c = w2.shape
    tr = _tile(r, 256, SUBLANE)
    tc = c if c <= 8192 else _tile(c, 8192)
    out = pl.pallas_call(
        _cast_kernel,
        grid=(r // tr, c // tc),
        in_specs=[pl.BlockSpec((tr, tc), lambda i, j: (i, j))],
        out_specs=pl.BlockSpec((tr, tc), lambda i, j: (i, j)),
        out_shape=jax.ShapeDtypeStruct((r, c), BF16),
        compiler_params=_cparams(("parallel", "parallel")),
        name="cast_bf16",
    )(w2)
    return out.reshape(shape)


def _mm_kernel(*refs, tn, dual, epilogue, rope_cfg):
    it = iter(refs)
    a_ref = next(it)
    w_ref = next(it)
    w2_ref = next(it) if dual else None
    res_ref = next(it) if epilogue in ("residual", "sglu") else None
    cos_ref = next(it) if epilogue == "rope" else None
    sin_ref = next(it) if epilogue == "rope" else None
    o_ref = next(it)

    a = a_ref[...]
    acc = jnp.dot(a, w_ref[...], preferred_element_type=F32)
    if dual:
        acc2 = jnp.dot(a, w2_ref[...], preferred_element_type=F32)

    if epilogue == "residual":
        o_ref[...] = (res_ref[...] + acc).astype(o_ref.dtype)
    elif epilogue == "swiglu":
        o_ref[...] = (jax.nn.silu(acc) * acc2).astype(o_ref.dtype)
    elif epilogue == "sglu":
        o_ref[...] = (res_ref[...] + acc * jax.nn.sigmoid(acc2)).astype(o_ref.dtype)
    elif epilogue == "rope":
        q_width, kv_width, kvw, hd = rope_cfg
        col0 = pl.program_id(1) * tn
        in_kv = (col0 >= q_width) & (col0 < q_width + kv_width)
        is_rope = (col0 < q_width) | (in_kv & (((col0 - q_width) // kvw) % 2 == 0))

        @pl.when(is_rope)
        def _():
            cos = cos_ref[...]
            sin = sin_ref[...]
            for c in range(tn // hd):
                x = acc[:, c * hd:(c + 1) * hd]
                o_ref[:, c * hd:(c + 1) * hd] = x * cos + pltpu.roll(x, hd // 2, 1) * sin

        @pl.when(jnp.logical_not(is_rope))
        def _():
            o_ref[...] = acc


def matmul(a, w, layer, *, tm, tn, epilogue, out_dtype=F32, n_out=None, w2_col_off=None, res=None, cos=None,
           sin=None, rope_cfg=None, a_buffers=2, a_time_major=None):
    m, kdim = a.shape
    n = w.shape[-1] if n_out is None else n_out
    dual = epilogue in ("swiglu", "sglu")
    assert m % tm == 0 and w.ndim == 3

    def w_spec(col_off):
        return pl.BlockSpec((None, kdim, tn), lambda i, j: (layer, 0, j + col_off))
    a_spec = pl.BlockSpec((tm, kdim), lambda i, j: (i, 0))
    if a_buffers != 2:
        a_spec = pl.BlockSpec((tm, kdim), lambda i, j: (i, 0), pipeline_mode=pl.Buffered(a_buffers))
    if a_time_major is not None:
        batch, t_len = a_time_major
        assert t_len % tm == 0 and batch * t_len == m
        nt = t_len // tm
        a = a.reshape(t_len, batch * kdim)
        a_spec = pl.BlockSpec((tm, kdim), lambda i, j: (i % nt, i // nt))
    in_specs = [a_spec, w_spec(0)]
    args = [a, w]
    if dual:
        assert w2_col_off % tn == 0 and n % tn == 0
        in_specs.append(w_spec(w2_col_off // tn))
        args.append(w)
    if epilogue in ("residual", "sglu"):
        in_specs.append(pl.BlockSpec((tm, tn), lambda i, j: (i, j)))
        args.append(res)
    if epilogue == "rope":
        hd = rope_cfg[3]
        in_specs += [pl.BlockSpec((tm, hd), lambda i, j: (i, 0))] * 2
        args += [cos, sin]
    return pl.pallas_call(
        functools.partial(_mm_kernel, tn=tn, dual=dual, epilogue=epilogue, rope_cfg=rope_cfg),
        grid=(m // tm, pl.cdiv(n, tn)),
        in_specs=in_specs,
        out_specs=pl.BlockSpec((tm, tn), lambda i, j: (i, j)),
        out_shape=jax.ShapeDtypeStruct((m, n), out_dtype),
        compiler_params=_cparams(("parallel", "arbitrary")),
        name="mm_" + epilogue,
    )(*args)


def _compress_proj_kernel(x_ref, pe_ref, w1_ref, w2_ref, o_ref, la_s, lb_s, *, hd, n_chunk, ncp):
    half = CMP_STRIDE * hd
    if ncp > n_chunk:
        la_s[...] = jnp.zeros_like(la_s)
        lb_s[...] = jnp.zeros_like(lb_s)
    for r in range(CMP_STRIDE):
        x = x_ref[pl.ds(r, n_chunk, stride=CMP_STRIDE), :]
        la_s[:n_chunk, r * hd:(r + 1) * hd] = x + pe_ref[r:r + 1, :]
        lb_s[:n_chunk, r * hd:(r + 1) * hd] = x + pe_ref[CMP_STRIDE + r:CMP_STRIDE + r + 1, :]
    ca = jnp.dot(la_s[...].astype(BF16), w1_ref[:half, :], preferred_element_type=F32)
    cb = jnp.dot(lb_s[...].astype(BF16), w1_ref[half:, :], preferred_element_type=F32)
    h = jax.nn.gelu(ca + pltpu.roll(cb, ncp - 1, 0))
    o_ref[...] = jnp.dot(h.astype(BF16), w2_ref[...].astype(BF16), preferred_element_type=F32).astype(o_ref.dtype)


def compress_proj(proj, pe, w1, w2, *, batch, t_len, n_kv, hd, col0, ncp):
    n_chunk = t_len // CMP_STRIDE
    assert t_len % CMP_STRIDE == 0 and ncp >= n_chunk and col0 % hd == 0
    hid = w1.shape[-1]
    kern = functools.partial(_compress_proj_kernel, hd=hd, n_chunk=n_chunk, ncp=ncp)
    return pl.pallas_call(
        kern,
        grid=(2, batch, n_kv),
        in_specs=[pl.BlockSpec((t_len, hd), lambda s, b, k: (b, col0 // hd + s * n_kv + k)),
                  pl.BlockSpec((None, CMP_LEN, hd), lambda s, b, k: (s, 0, 0)),
                  pl.BlockSpec((None, CMP_LEN * hd, hid), lambda s, b, k: (s, 0, 0)),
                  pl.BlockSpec((None, hid, hd), lambda s, b, k: (s, 0, 0))],
        out_specs=pl.BlockSpec((None, None, None, ncp, hd), lambda s, b, k: (s, b, k, 0, 0)),
        out_shape=jax.ShapeDtypeStruct((2, batch, n_kv, ncp, hd), BF16),
        scratch_shapes=[pltpu.VMEM((ncp, CMP_STRIDE * hd), F32), pltpu.VMEM((ncp, CMP_STRIDE * hd), F32)],
        compiler_params=_cparams(("arbitrary", "arbitrary", "arbitrary")),
        name="compress_proj",
    )(proj, pe, w1, w2)


def _overlap_matrix(ncp, nsp):
    ic = jnp.arange(ncp, dtype=jnp.int32)[:, None] * CMP_STRIDE
    js = jnp.arange(nsp, dtype=jnp.int32)[None, :] * SEL_BLOCK
    return ((ic < js + SEL_BLOCK) & (ic + CMP_LEN > js)).astype(F32)


def _compress_paged_kernel(*refs, pg, page, hd, n_kv, n_groups, n_chunk, ncp):
    pt_ref = refs[0]
    page_refs = refs[1:1 + pg]
    pe_ref, w1_ref, w2_ref, o_ref, la_s, lb_s, tail_s = refs[1 + pg:]
    del pt_ref, n_groups
    g = pl.program_id(1)
    cpp = page // CMP_STRIDE
    rows_g = pg * cpp
    half = CMP_STRIDE * hd
    tail = 2 * SUBLANE

    if ncp > n_chunk:
        @pl.when(g == 0)
        def _():
            o_ref[...] = jnp.zeros_like(o_ref)

    for i in range(pg):
        for r in range(CMP_STRIDE):
            x = page_refs[i][pl.ds(r, cpp, stride=CMP_STRIDE), :, :]
            xt = jnp.swapaxes(x, 0, 1)
            for j in range(2 * n_kv):
                s, k = divmod(j, n_kv)
                rows = slice(k * rows_g + i * cpp, k * rows_g + (i + 1) * cpp)
                la_s[s, rows, r * hd:(r + 1) * hd] = xt[j] + pe_ref[s, r:r + 1, :]
                lb_s[s, rows, r * hd:(r + 1) * hd] = xt[j] + pe_ref[s, CMP_STRIDE + r:CMP_STRIDE + r + 1, :]

    row0 = pl.multiple_of(g * rows_g, rows_g)
    last = lax.broadcasted_iota(jnp.int32, (tail, 1), 0) == tail - 1
    for s in range(2):
        ca = jnp.dot(la_s[s].astype(BF16), w1_ref[s, :half, :], preferred_element_type=F32)
        cb = jnp.dot(lb_s[s].astype(BF16), w1_ref[s, half:, :], preferred_element_type=F32)
        w2 = w2_ref[s].astype(BF16)
        for k in range(n_kv):
            j = s * n_kv + k
            ca_k = ca[k * rows_g:(k + 1) * rows_g]
            cb_k = cb[k * rows_g:(k + 1) * rows_g]

            @pl.when(g > 0)
            def _(j=j, s=s, k=k, cb_k=cb_k, w2=w2):
                h = jax.nn.gelu(tail_s[j] + cb_k[0:1, :])
                o_new = jnp.dot(h.astype(BF16), w2, preferred_element_type=F32)
                prev = pl.ds(row0 - tail, tail)
                o_old = o_ref[s, k, prev, :].astype(F32)
                o_ref[s, k, prev, :] = jnp.where(last, o_new, o_old).astype(o_ref.dtype)

            h = jax.nn.gelu(ca_k + pltpu.roll(cb_k, rows_g - 1, 0))
            o_ref[s, k, pl.ds(row0, rows_g), :] = jnp.dot(h.astype(BF16), w2,
                                                          preferred_element_type=F32).astype(o_ref.dtype)
            tail_s[j] = ca_k[rows_g - tail:, :]


def compress_paged(page_table, cache4, pe, w1, w2, *, n_kv, hd, ncp):
    bd, n_pages = page_table.shape
    page = cache4.shape[1]
    assert page % CMP_STRIDE == 0
    cpp = page // CMP_STRIDE
    n_chunk = n_pages * cpp
    assert ncp >= n_chunk and cpp == SUBLANE and 2 * n_kv == SUBLANE
    pg = _tile(n_pages, 16, 1)
    n_groups = n_pages // pg
    hid = w1.shape[-1]

    def page_spec(i):
        return pl.BlockSpec((None, page, 2 * n_kv, hd), lambda b, g, pt: (pt[b * n_pages + g * pg + i], 0, 0, 0))

    kern = functools.partial(_compress_paged_kernel, pg=pg, page=page, hd=hd, n_kv=n_kv, n_groups=n_groups,
                             n_chunk=n_chunk, ncp=ncp)
    grid_spec = pltpu.PrefetchScalarGridSpec(
        num_scalar_prefetch=1,
        grid=(bd, n_groups),
        in_specs=[page_spec(i) for i in range(pg)] + [
            pl.BlockSpec((2, CMP_LEN, hd), lambda b, g, pt: (0, 0, 0)),
            pl.BlockSpec((2, CMP_LEN * hd, hid), lambda b, g, pt: (0, 0, 0)),
            pl.BlockSpec((2, hid, hd), lambda b, g, pt: (0, 0, 0))],
        out_specs=pl.BlockSpec((2, None, n_kv, ncp, hd), lambda b, g, pt: (0, b, 0, 0, 0)),
        scratch_shapes=[pltpu.VMEM((2, n_kv * pg * cpp, CMP_STRIDE * hd), F32),
                        pltpu.VMEM((2, n_kv * pg * cpp, CMP_STRIDE * hd), F32),
                        pltpu.VMEM((2 * n_kv, 2 * SUBLANE, hid), F32)],
    )
    return pl.pallas_call(
        kern,
        grid_spec=grid_spec,
        out_shape=jax.ShapeDtypeStruct((2, bd, n_kv, ncp, hd), BF16),
        compiler_params=_cparams(("arbitrary", "arbitrary")),
        name="compress_paged",
    )(page_table.reshape(-1), *([cache4] * pg), pe, w1, w2)


def _nsa_prompt_kernel(q_ref, ks_ref, vs_ref, kw_ref, vw_ref, kc_ref, vc_ref, gate_ref, ovl_ref, blk_ref,
                       o_ref, ks_s, vs_s, kw_s, vw_s, acc_s, mx_s, *, gqa, hd, t_len, n_cmp, n_sel, k_sel,
                       sel_ext, scale):
    qb = pl.program_id(2)

    @pl.when(qb == 0)
    def _():
        ks_s[:, :hd] = ks_ref[...].astype(BF16)
        ks_s[:, hd:] = blk_ref[...]
        vs_s[:, :hd] = vs_ref[...].astype(BF16)
        vs_s[:, hd:] = jnp.ones((t_len, LANE), BF16)
        kw_s[...] = kw_ref[...].astype(BF16)
        vw_s[:, :hd] = vw_ref[...].astype(BF16)
        vw_s[:, hd:] = jnp.ones((t_len, LANE), BF16)

    rows = gqa * Q_BLK
    b0 = qb * Q_BLK
    q = q_ref[...]
    qg = jnp.concatenate([q[:, g * hd:(g + 1) * hd] for g in range(gqa)], axis=0).astype(BF16)
    pos_q = b0 + lax.broadcasted_iota(jnp.int32, (Q_BLK, 1), 0)

    c_exp = scale * LOG2_E

    def exp_shifted(s3):
        return jnp.exp2((s3 - jnp.max(s3, axis=-1, keepdims=True)) * c_exp)

    ncp = kc_ref.shape[0]
    s = lax.dot_general(qg, kc_ref[...], _NT, preferred_element_type=F32)
    ci = lax.broadcasted_iota(jnp.int32, (1, ncp), 1)
    m_cmp = ((ci * CMP_STRIDE + (CMP_LEN - 1)) <= pos_q) & (ci < n_cmp)
    s3 = s.reshape(gqa, Q_BLK, ncp) + jnp.where(m_cmp, 0.0, NEG_INF)[None]
    e3 = exp_shifted(s3) * m_cmp.astype(F32)[None]
    l3 = jnp.sum(e3, axis=-1, keepdims=True)
    p3 = e3 * jnp.where(l3 > 0.0, 1.0 / l3, 0.0)
    o_cmp = jnp.dot(p3.reshape(rows, ncp).astype(BF16), vc_ref[...], preferred_element_type=F32)

    nsp = ovl_ref.shape[1]
    imp = jnp.dot(jnp.sum(p3, axis=0), ovl_ref[...], preferred_element_type=F32,
                  precision=lax.Precision.HIGHEST)
    jj = lax.broadcasted_iota(jnp.int32, (1, nsp), 1)
    cur = pos_q // SEL_BLOCK
    forced = (jj == 0) | (jj == cur) | (jj == cur - 1)
    future = jj * SEL_BLOCK > pos_q
    score = jnp.where(future, -jnp.inf, jnp.where(forced, jnp.inf, imp))
    rank = jnp.zeros((Q_BLK, nsp), F32)
    for j2 in range(n_sel):
        col = score[:, j2:j2 + 1]
        beats = (col > score) | ((col == score) & (j2 < jj))
        rank = rank + beats.astype(F32)
    sel_ok = (rank < k_sel) & (jj < qb)
    nb = jnp.where(sel_ok, 0.0, NEG_INF).astype(BF16)
    q_aug = jnp.concatenate([qg, jnp.concatenate([nb] * gqa, axis=0)], axis=1)

    tq = lax.broadcasted_iota(jnp.int32, (Q_BLK, 1), 0)
    bias_d = jnp.where(lax.broadcasted_iota(jnp.int32, (1, SEL_BLOCK), 1) <= tq, 0.0, NEG_INF)
    diag = pl.ds(pl.multiple_of(b0, SEL_BLOCK), SEL_BLOCK)
    s_d = (lax.dot_general(qg, ks_s[diag, :hd], _NT, preferred_element_type=F32)
           + jnp.concatenate([bias_d] * gqa, axis=0))
    m_d = jnp.max(s_d, axis=-1, keepdims=True)

    def prefix(kv_len):
        s = lax.dot_general(q_aug, ks_s[:kv_len, :], _NT, preferred_element_type=F32)
        mx = jnp.maximum(jnp.max(s, axis=-1, keepdims=True), m_d)
        e = jnp.exp2((s - mx) * c_exp).astype(BF16)
        acc_s[...] = jnp.dot(e, vs_s[:kv_len, :], preferred_element_type=F32)
        mx_s[...] = mx

    def no_prefix():
        acc_s[...] = jnp.zeros_like(acc_s)
        mx_s[...] = m_d

    n_ext = (b0 + sel_ext - 1) // sel_ext
    pl.when(n_ext == 0)(no_prefix)
    for c in range(1, t_len // sel_ext + 1):
        pl.when(n_ext == c)(functools.partial(prefix, c * sel_ext))
    e_d = jnp.exp2((s_d - mx_s[...]) * c_exp).astype(BF16)
    acc = acc_s[...] + jnp.dot(e_d, vs_s[diag, :], preferred_element_type=F32)
    o_sel = acc[:, :hd] / acc[:, hd:hd + 1]

    wk = WINDOW + Q_BLK
    start = pl.multiple_of(jnp.maximum(b0 - WINDOW, 0), Q_BLK)
    s = lax.dot_general(qg, kw_s[pl.ds(start, wk), :], _NT, preferred_element_type=F32)
    dist = pos_q - (start + lax.broadcasted_iota(jnp.int32, (1, wk), 1))
    m_win = (dist >= 0) & (dist <= WINDOW)
    s3 = s.reshape(gqa, Q_BLK, wk) + jnp.where(m_win, 0.0, NEG_INF)[None]
    e3 = exp_shifted(s3)
    acc_w = jnp.dot(e3.reshape(rows, wk).astype(BF16), vw_s[pl.ds(start, wk), :], preferred_element_type=F32)
    o_win = acc_w[:, :hd] / acc_w[:, hd:hd + 1]

    gate = jax.nn.sigmoid(gate_ref[...])
    for g in range(gqa):
        r = slice(g * Q_BLK, (g + 1) * Q_BLK)
        o = (gate[:, 3 * g:3 * g + 1] * o_cmp[r] + gate[:, 3 * g + 1:3 * g + 2] * o_sel[r]
             + gate[:, 3 * g + 2:3 * g + 3] * o_win[r])
        o_ref[:, g * hd:(g + 1) * hd] = o.astype(o_ref.dtype)


def nsa_prompt_attention(proj, gates_r, kc, vc, *, batch, t_len, n_kv, gqa, hd, n_cmp):
    m = batch * t_len
    nq = t_len // Q_BLK
    q_width = n_kv * gqa * hd
    kvw = n_kv * hd
    ncp = kc.shape[2]
    n_sel = -(-t_len // SEL_BLOCK)
    nsp = -(-n_sel // LANE) * LANE
    k_sel = min(N_SELECT, n_sel)
    assert t_len % Q_BLK == 0 and t_len >= WINDOW + Q_BLK and Q_BLK == SEL_BLOCK
    ovl = _overlap_matrix(ncp, nsp)
    blk_onehot = (jnp.arange(t_len, dtype=jnp.int32)[:, None] // SEL_BLOCK
                  == jnp.arange(nsp, dtype=jnp.int32)[None, :]).astype(BF16)

    def kv_spec(slot):
        base = (q_width + slot * kvw) // hd
        return pl.BlockSpec((t_len, hd), lambda b, k, i: (b, base + k))

    sel_ext = _tile(t_len, max(t_len // 4, LANE))
    kern = functools.partial(_nsa_prompt_kernel, gqa=gqa, hd=hd, t_len=t_len, n_cmp=n_cmp, n_sel=n_sel,
                             k_sel=k_sel, sel_ext=sel_ext, scale=hd ** -0.5)
    return pl.pallas_call(
        kern,
        grid=(batch, n_kv, nq),
        in_specs=[pl.BlockSpec((Q_BLK, gqa * hd), lambda b, k, i: (b * nq + i, k)),
                  kv_spec(2), kv_spec(3), kv_spec(4), kv_spec(5),
                  pl.BlockSpec((None, None, ncp, hd), lambda b, k, i: (b, k, 0, 0)),
                  pl.BlockSpec((None, None, ncp, hd), lambda b, k, i: (b, k, 0, 0)),
                  pl.BlockSpec((None, Q_BLK, 3 * gqa), lambda b, k, i: (k, b * nq + i, 0)),
                  pl.BlockSpec((ncp, nsp), lambda b, k, i: (0, 0)),
                  pl.BlockSpec((t_len, nsp), lambda b, k, i: (0, 0))],
        out_specs=pl.BlockSpec((Q_BLK, gqa * hd), lambda b, k, i: (b * nq + i, k)),
        out_shape=jax.ShapeDtypeStruct((m, q_width), BF16),
        scratch_shapes=[pltpu.VMEM((t_len, hd + nsp), BF16), pltpu.VMEM((t_len, hd + LANE), BF16),
                        pltpu.VMEM((t_len, hd), BF16), pltpu.VMEM((t_len, hd + LANE), BF16),
                        pltpu.VMEM((gqa * Q_BLK, hd + LANE), F32), pltpu.VMEM((gqa * Q_BLK, 1), F32)],
        compiler_params=_cparams(("parallel", "parallel", "arbitrary")),
        name="nsa_prompt",
    )(proj, proj, proj, proj, proj, kc, vc, gates_r, ovl, blk_onehot)


def _nsa_dec_select_kernel(q_ref, kc_ref, vc_ref, ovl_ref, ocmp_ref, idx_ref, *, n_cmp, n_sel, pos_q, scale):
    q = q_ref[...].astype(BF16)
    ncp = kc_ref.shape[0]
    nsp = ovl_ref.shape[1]
    s = lax.dot_general(q, kc_ref[...], _NT, preferred_element_type=F32) * scale
    ci = lax.broadcasted_iota(jnp.int32, (1, ncp), 1)
    m_cmp = ((ci * CMP_STRIDE + (CMP_LEN - 1)) <= pos_q) & (ci < n_cmp)
    s = jnp.where(m_cmp, s, NEG_INF)
    e = jnp.where(m_cmp, jnp.exp(s - jnp.max(s, axis=-1, keepdims=True)), 0.0)
    l = jnp.sum(e, axis=-1, keepdims=True)
    p = e * jnp.where(l > 0.0, 1.0 / l, 0.0)
    ocmp_ref[...] = jnp.dot(p.astype(BF16), vc_ref[...], preferred_element_type=F32)

    imp = jnp.sum(jnp.dot(p, ovl_ref[...], preferred_element_type=F32, precision=lax.Precision.HIGHEST),
                  axis=0, keepdims=True)
    jj = lax.broadcasted_iota(jnp.int32, (1, nsp), 1)
    cur = pos_q // SEL_BLOCK
    forced = (jj == 0) | (jj == cur) | (jj == cur - 1)
    future = jj * SEL_BLOCK > pos_q
    score = jnp.where(future, -jnp.inf, jnp.where(forced, jnp.inf, imp))
    rr = lax.broadcasted_iota(jnp.int32, (nsp, nsp), 0)
    cc = lax.broadcasted_iota(jnp.int32, (nsp, nsp), 1)
    score_b = jnp.broadcast_to(score, (nsp, nsp))
    score_col = jnp.sum(jnp.where(rr == cc, score_b, 0.0), axis=1, keepdims=True)
    beaten = ((score_b > score_col) | ((score_b == score_col) & (cc < rr))) & (cc < n_sel)
    rank_col = jnp.sum(beaten.astype(F32), axis=1, keepdims=True)
    lane = lax.broadcasted_iota(jnp.int32, (nsp, LANE), 1)
    row = lax.broadcasted_iota(jnp.int32, (nsp, LANE), 0)
    hit = (rank_col == lane.astype(F32)) & (row < n_sel)
    idx_ref[...] = jnp.sum(jnp.where(hit, row.astype(F32), 0.0), axis=0, keepdims=True).astype(jnp.int32)


def nsa_dec_select(q, kc, vc, *, n_cmp, n_sel, pos_q):
    bd, n_kv, gqa, hd = q.shape
    ncp = kc.shape[2]
    nsp = -(-n_sel // LANE) * LANE
    kern = functools.partial(_nsa_dec_select_kernel, n_cmp=n_cmp, n_sel=n_sel, pos_q=pos_q, scale=hd ** -0.5)
    return pl.pallas_call(
        kern,
        grid=(bd, n_kv),
        in_specs=[pl.BlockSpec((None, None, gqa, hd), lambda b, k: (b, k, 0, 0)),
                  pl.BlockSpec((None, None, ncp, hd), lambda b, k: (b, k, 0, 0)),
                  pl.BlockSpec((None, None, ncp, hd), lambda b, k: (b, k, 0, 0)),
                  pl.BlockSpec((ncp, nsp), lambda b, k: (0, 0))],
        out_specs=[pl.BlockSpec((None, None, gqa, hd), lambda b, k: (b, k, 0, 0)),
                   pl.BlockSpec((None, None, 1, LANE), lambda b, k: (b, k, 0, 0))],
        out_shape=[jax.ShapeDtypeStruct((bd, n_kv, gqa, hd), F32),
                   jax.ShapeDtypeStruct((bd, n_kv, 1, LANE), jnp.int32)],
        compiler_params=_cparams(("parallel", "parallel")),
        name="nsa_dec_select",
    )(q, kc, vc, _overlap_matrix(ncp, nsp))


def _nsa_dec_attend_kernel(*refs, k_sel, n_kv, past_len, pos_q, scale):
    pt_ref, idx_ref, q_ref = refs[:3]
    kv_refs = refs[3:3 + k_sel]
    new_ref, kw_ref, vw_ref, ocmp_ref, gate_ref, o_ref = refs[3 + k_sel:]
    del pt_ref
    b = pl.program_id(0)
    k = pl.program_id(1)

    q = q_ref[...].astype(BF16)
    new = new_ref[...].astype(BF16)
    newf = new.astype(F32)
    s_all = lax.dot_general(q, new, _NT, preferred_element_type=F32) * scale

    first = (lax.broadcasted_iota(jnp.int32, (SEL_BLOCK, 1), 0) == 0).astype(F32)
    new_k = first * new_ref[2:3, :]
    new_v = first * new_ref[3:4, :]
    off = lax.broadcasted_iota(jnp.int32, (1, SEL_BLOCK), 1)
    s_blk, v_blk = [], []
    for r in range(k_sel):
        tok0 = idx_ref[(b * n_kv + k) * k_sel + r] * SEL_BLOCK
        is_new = tok0 >= past_len
        kb = jnp.where(is_new, new_k, kv_refs[r][:, k, :]).astype(BF16)
        v_blk.append(jnp.where(is_new, new_v, kv_refs[r][:, n_kv + k, :]).astype(BF16))
        s = lax.dot_general(q, kb, _NT, preferred_element_type=F32) * scale
        s_blk.append(jnp.where(tok0 + off <= pos_q, s, NEG_INF))
    mx = functools.reduce(jnp.maximum, [jnp.max(s, axis=-1, keepdims=True) for s in s_blk])
    l_sel = jnp.zeros_like(mx)
    acc = jnp.zeros((q.shape[0], q.shape[1]), F32)
    for s, vb in zip(s_blk, v_blk):
        e = jnp.exp(s - mx)
        l_sel = l_sel + jnp.sum(e, axis=-1, keepdims=True)
        acc = acc + jnp.dot(e.astype(BF16), vb, preferred_element_type=F32)
    o_sel = acc / l_sel

    wb = kw_ref.shape[0]
    s = lax.dot_general(q, kw_ref[...].astype(BF16), _NT, preferred_element_type=F32) * scale
    pos_w = (past_len - wb) + lax.broadcasted_iota(jnp.int32, (1, wb), 1)
    dist = pos_q - pos_w
    m_win = (dist >= 0) & (dist <= WINDOW) & (pos_w >= 0)
    s = jnp.where(m_win, s, NEG_INF)
    s_nw = s_all[:, 4:5]
    mx = jnp.maximum(jnp.max(s, axis=-1, keepdims=True), s_nw)
    e = jnp.where(m_win, jnp.exp(s - mx), 0.0)
    e_nw = jnp.exp(s_nw - mx)
    l_w = jnp.sum(e, axis=-1, keepdims=True) + e_nw
    o_win = (jnp.dot(e.astype(BF16), vw_ref[...].astype(BF16), preferred_element_type=F32)
             + e_nw.astype(BF16).astype(F32) * newf[5:6, :]) / l_w

    gate = jax.nn.sigmoid(gate_ref[...])
    o = gate[:, 0:1] * ocmp_ref[...] + gate[:, 1:2] * o_sel + gate[:, 2:3] * o_win
    o_ref[...] = o.astype(o_ref.dtype)


def nsa_dec_attend(page_table, idx, q, cache4, new_rows, win2d, o_cmp, gates, *, k_sel, past_len, pos_q):
    bd, n_kv, gqa, hd = q.shape
    n_pages = page_table.shape[1]
    page = cache4.shape[1]
    wb = win2d.shape[1]
    assert page % SEL_BLOCK == 0 and past_len % SEL_BLOCK == 0 and pos_q == past_len
    assert 2 * n_kv == SUBLANE and cache4.shape[2] == 4 * n_kv
    bpp = page // SEL_BLOCK
    kern = functools.partial(_nsa_dec_attend_kernel, k_sel=k_sel, n_kv=n_kv, past_len=past_len, pos_q=pos_q,
                             scale=hd ** -0.5)

    def blk_spec(r):
        def index_map(b, k, pt, ix):
            j = ix[(b * n_kv + k) * k_sel + r]
            return pt[b * n_pages + jnp.minimum(j // bpp, n_pages - 1)], j % bpp, 1, 0

        return pl.BlockSpec((None, SEL_BLOCK, 2 * n_kv, hd), index_map)

    grid_spec = pltpu.PrefetchScalarGridSpec(
        num_scalar_prefetch=2,
        grid=(bd, n_kv),
        in_specs=[pl.BlockSpec((None, None, gqa, hd), lambda b, k, pt, ix: (b, k, 0, 0))]
        + [blk_spec(r) for r in range(k_sel)]
        + [pl.BlockSpec((None, None, SUBLANE, hd), lambda b, k, pt, ix: (b, k, 0, 0)),
           pl.BlockSpec((None, wb, hd), lambda b, k, pt, ix: (b, 0, k)),
           pl.BlockSpec((None, wb, hd), lambda b, k, pt, ix: (b, 0, n_kv + k)),
           pl.BlockSpec((None, None, gqa, hd), lambda b, k, pt, ix: (b, k, 0, 0)),
           pl.BlockSpec((None, None, gqa, 3), lambda b, k, pt, ix: (b, k, 0, 0))],
        out_specs=pl.BlockSpec((None, None, gqa, hd), lambda b, k, pt, ix: (b, k, 0, 0)),
    )
    return pl.pallas_call(
        kern,
        grid_spec=grid_spec,
        out_shape=jax.ShapeDtypeStruct((bd, n_kv, gqa, hd), BF16),
        compiler_params=_cparams(("arbitrary", "arbitrary")),
        name="nsa_dec_attend",
    )(page_table.reshape(-1), idx, q, *([cache4] * k_sel), new_rows, win2d, win2d, o_cmp, gates)


N_POW = SUBLANE


def _s5_disc_kernel(lre_ref, lim_ref, ldt_ref, bre_ref, bim_ref, bbre_ref, bbim_ref, pre_ref, pim_ref):
    lam_re = lre_ref[...]
    lam_im = lim_ref[...]
    dt = jnp.exp(ldt_ref[...])
    mag = jnp.exp(lam_re * dt)
    ab_re = mag * jnp.cos(lam_im * dt)
    ab_im = mag * jnp.sin(lam_im * dt)
    nr, ni = ab_re - 1.0, ab_im
    den = lam_re * lam_re + lam_im * lam_im
    f_re = (nr * lam_re + ni * lam_im) / den
    f_im = (ni * lam_re - nr * lam_im) / den
    b_re = bre_ref[...]
    b_im = bim_ref[...]
    bbre_ref[...] = f_re * b_re - f_im * b_im
    bbim_ref[...] = f_re * b_im + f_im * b_re
    pr, pi = ab_re, ab_im
    pre_ref[0] = pr
    pim_ref[0] = pi
    for n in range(1, N_POW):
        pr, pi = pr * ab_re - pi * ab_im, pr * ab_im + pi * ab_re
        pre_ref[n] = pr
        pim_ref[n] = pi


def s5_discretize(lam_re, lam_im, log_dt, b_re, b_im):
    g, n, c = b_re.shape
    rep = lambda x: jnp.broadcast_to(x[:, None, :], (g, c, n)).reshape(g * c, n)
    ldt = jnp.broadcast_to(log_dt[:, None, None], (g, c, n)).reshape(g * c, n)
    bt = lambda x: x.transpose(0, 2, 1).reshape(g * c, n)
    full = pl.BlockSpec((g * c, n), lambda: (0, 0))
    pw = pl.BlockSpec((N_POW, g * c, n), lambda: (0, 0, 0))
    bbre, bbim, pre, pim = pl.pallas_call(
        _s5_disc_kernel,
        in_specs=[full] * 5,
        out_specs=[full, full, pw, pw],
        out_shape=[jax.ShapeDtypeStruct((g * c, n), F32)] * 2
        + [jax.ShapeDtypeStruct((N_POW, g * c, n), F32)] * 2,
        compiler_params=pltpu.CompilerParams(vmem_limit_bytes=VMEM_LIMIT_BYTES),
        name="s5_discretize",
    )(rep(lam_re), rep(lam_im), ldt, bt(b_re), bt(b_im))
    pick = lambda x: x.reshape(N_POW, g, c, n)[:, :, 0, :]
    return bbre.reshape(g, c, n), bbim.reshape(g, c, n), pick(pre), pick(pim)


def _s5_scan_kernel(u_ref, h0r_ref, h0i_ref, bbr_ref, bbi_ref, ccr_ref, cci_ref, d_ref, mr_ref, mi_ref,
                    pr_ref, pi_ref, z_ref, hr_ref, hi_ref, cr_s, ci_s, *, batch, n_groups, n_sub, n_t):
    t = pl.program_id(1)
    shifts = [batch << n for n in range(8) if (batch << n) < SUBLANE]
    row = lax.broadcasted_iota(jnp.int32, (SUBLANE, 1), 0)

    @pl.when(t == 0)
    def _():
        cr_s[...] = h0r_ref[...]
        ci_s[...] = h0i_ref[...]

    bbr = bbr_ref[...].astype(BF16)
    bbi = bbi_ref[...].astype(BF16)
    ccr = ccr_ref[...].astype(BF16)
    cci = cci_ref[...].astype(BF16)
    pr = pr_ref[...]
    pi = pi_ref[...]
    cr, ci = cr_s[...], ci_s[...]

    rows_c = n_groups * SUBLANE // n_sub
    for c in range(n_sub):
        rs = slice(c * rows_c, (c + 1) * rows_c)
        u = u_ref[rs, :]
        ub = u.astype(BF16)
        xr_all = jnp.dot(ub, bbr, preferred_element_type=F32)
        xi_all = jnp.dot(ub, bbi, preferred_element_type=F32)
        hr_parts, hi_parts = [], []
        for r in range(rows_c // SUBLANE):
            xr = xr_all[r * SUBLANE:(r + 1) * SUBLANE]
            xi = xi_all[r * SUBLANE:(r + 1) * SUBLANE]
            for n, sh in enumerate(shifts):
                mr = mr_ref[n]
                mi = mi_ref[n]
                sr = pltpu.roll(xr, sh, 0)
                si = pltpu.roll(xi, sh, 0)
                xr, xi = xr + (mr * sr - mi * si), xi + (mr * si + mi * sr)
            xr, xi = xr + (pr * cr - pi * ci), xi + (pr * ci + pi * cr)
            hr_parts.append(xr)
            hi_parts.append(xi)
            cr, ci = xr, xi
            for sh in shifts:
                keep = row >= SUBLANE - sh
                cr = jnp.where(keep, cr, pltpu.roll(cr, SUBLANE - sh, 0))
                ci = jnp.where(keep, ci, pltpu.roll(ci, SUBLANE - sh, 0))
        h_re = jnp.concatenate(hr_parts, axis=0).astype(BF16)
        h_im = jnp.concatenate(hi_parts, axis=0).astype(BF16)
        y = (jnp.dot(h_re, ccr, preferred_element_type=F32) - jnp.dot(h_im, cci, preferred_element_type=F32)
             + d_ref[...] * u)
        z_ref[rs, :] = jax.nn.gelu(y).astype(z_ref.dtype)
    cr_s[...] = cr
    ci_s[...] = ci

    @pl.when(t == n_t - 1)
    def _():
        hr_ref[...] = cr
        hi_ref[...] = ci


def s5_scan(u_tb, h0_re, h0_im, prm, *, batch, t_len):
    bb_re, bb_im, cc_re, cc_im, d_skip, pw_re, pw_im = prm
    n_slab, cw, sw = bb_re.shape
    assert SUBLANE % batch == 0
    tpg = SUBLANE // batch
    assert t_len % tpg == 0
    tt = _tile(t_len, 128, tpg)
    n_t = t_len // tt
    n_groups = tt // tpg
    shifts = [batch << n for n in range(8) if (batch << n) < SUBLANE]
    row = jnp.arange(SUBLANE)[None, :, None]

    def shift_tab(pw):
        tabs = [jnp.where(row >= sh, pw[sh // batch - 1][:, None, :], 0.0) for sh in shifts]
        return jnp.stack(tabs, axis=1) if tabs else jnp.zeros((n_slab, 1, SUBLANE, sw), F32)

    carry_tab = lambda pw: pw[jnp.arange(SUBLANE) // batch].transpose(1, 0, 2)
    tile_rows = lambda h: jnp.tile(h.reshape(batch, n_slab, sw).transpose(1, 0, 2), (1, tpg, 1))
    n_tab = max(len(shifts), 1)

    slab3 = lambda r, c: pl.BlockSpec((None, r, c), lambda s, t: (s, 0, 0))
    tab4 = pl.BlockSpec((None, n_tab, SUBLANE, sw), lambda s, t: (s, 0, 0, 0))
    n_sub = max(1, n_groups * SUBLANE // LANE)
    assert n_groups % n_sub == 0
    kern = functools.partial(_s5_scan_kernel, batch=batch, n_groups=n_groups, n_sub=n_sub, n_t=n_t)
    z, hr, hi = pl.pallas_call(
        kern,
        grid=(n_slab, n_t),
        in_specs=[pl.BlockSpec((tt * batch, cw), lambda s, t: (t, s)),
                  slab3(SUBLANE, sw), slab3(SUBLANE, sw),
                  slab3(cw, sw), slab3(cw, sw), slab3(sw, cw), slab3(sw, cw),
                  pl.BlockSpec((1, cw), lambda s, t: (0, s)),
                  tab4, tab4,
                  slab3(SUBLANE, sw), slab3(SUBLANE, sw)],
        out_specs=[pl.BlockSpec((tt * batch, cw), lambda s, t: (t, s)), slab3(SUBLANE, sw), slab3(SUBLANE, sw)],
        out_shape=[jax.ShapeDtypeStruct((t_len * batch, n_slab * cw), BF16),
                   jax.ShapeDtypeStruct((n_slab, SUBLANE, sw), F32),
                   jax.ShapeDtypeStruct((n_slab, SUBLANE, sw), F32)],
        scratch_shapes=[pltpu.VMEM((SUBLANE, sw), F32), pltpu.VMEM((SUBLANE, sw), F32)],
        compiler_params=_cparams(("parallel", "arbitrary")),
        name="s5_scan",
    )(u_tb, tile_rows(h0_re), tile_rows(h0_im), bb_re, bb_im, cc_re, cc_im, d_skip.reshape(1, -1),
      shift_tab(pw_re), shift_tab(pw_im), carry_tab(pw_re), carry_tab(pw_im))
    final = lambda h: h[:, :batch].transpose(1, 0, 2).reshape(h0_re.shape)
    return z, final(hr), final(hi)


def s5_prepare(lam_re, lam_im, log_dt, b_re, b_im, c_re, c_im, d_skip):
    g, n, c = b_re.shape
    gps = LANE // c
    n_slab = g // gps
    sw = gps * n
    bbt_re, bbt_im, pw_re, pw_im = s5_discretize(lam_re, lam_im, log_dt, b_re, b_im)
    eye = jnp.eye(gps, dtype=F32)

    def blockdiag_in(bt):
        x = bt.reshape(n_slab, gps, c, 1, n) * eye[None, :, None, :, None]
        return x.reshape(n_slab, gps * c, sw)

    def blockdiag_out(cm):
        x = cm.transpose(0, 2, 1).reshape(n_slab, gps, n, 1, c) * eye[None, :, None, :, None]
        return x.reshape(n_slab, sw, gps * c)

    return (blockdiag_in(bbt_re), blockdiag_in(bbt_im), blockdiag_out(c_re.astype(F32)),
            blockdiag_out(c_im.astype(F32)), d_skip.astype(F32), pw_re.reshape(N_POW, n_slab, sw),
            pw_im.reshape(N_POW, n_slab, sw))


MM_ROWS = 1024


def _rope_tables(pos, hd):
    half = hd // 2
    inv = ROPE_THETA ** (-jnp.arange(half, dtype=F32) / half)
    ang = pos.astype(F32)[:, None] * inv[None, :]
    cos = jnp.cos(ang)
    sin = jnp.sin(ang)
    return jnp.concatenate([cos, cos], axis=-1), jnp.concatenate([-sin, sin], axis=-1)


def in_proj(u, w_in, layer, pos, *, q_width, kv_width, kvw, hd):
    m = u.shape[0]
    cos, sin = _rope_tables(pos, hd)
    return matmul(u, w_in, layer, tm=_tile(m, MM_ROWS, SUBLANE), tn=kvw, epilogue="rope", cos=cos, sin=sin,
                  rope_cfg=(q_width, kv_width, kvw, hd))


def out_proj(o, w, layer, res):
    m, d = res.shape
    return matmul(o, w, layer, tm=_tile(m, MM_ROWS, SUBLANE), tn=_tile(d, 512), epilogue="residual", res=res)


def glu_proj(z, w_glu, layer, res, time_major=None):
    m, d = res.shape
    rows = m if time_major is None else time_major[1]
    return matmul(z, w_glu, layer, tm=_tile(rows, MM_ROWS, SUBLANE), tn=_tile(d, 512), epilogue="sglu",
                  n_out=d, w2_col_off=d, res=res, a_time_major=time_major)


def ffn(h, hn, w_gu, w_down, layer):
    m, d = h.shape
    d_ff = w_down.shape[1]
    tm = _tile(m, MM_ROWS, SUBLANE)
    act = matmul(hn, w_gu, layer, tm=tm, tn=_tile(d_ff, 512), epilogue="swiglu", out_dtype=BF16, n_out=d_ff,
                 w2_col_off=d_ff)
    return matmul(act, w_down, layer, tm=tm, tn=_tile(d, 256), epilogue="residual", res=h, a_buffers=1)


def kernel(x_prompt, x_sample, cache_nsa, cache_win, state_s5_re, state_s5_im, page_table, norm_mix, norm_ffn,
           norm_final, att_w_in, att_w_out, cmp_pe, cmp_w1, cmp_w2, s5_lambda_re, s5_lambda_im, s5_log_dt,
           s5_b_re, s5_b_im, s5_c_re, s5_c_im, s5_d, s5_w_glu, ffn_w_gate_up, ffn_w_down):
    batch, t_len, d = x_prompt.shape
    bd, ts, _ = x_sample.shape
    _, n_pool, page, _, n_kv, hd = cache_nsa.shape
    wb = cache_win.shape[2]
    n_pages = page_table.shape[1]
    past_len = n_pages * page
    depth = norm_mix.shape[0]
    n_heads = d // hd
    gqa = n_heads // n_kv
    q_width = n_heads * hd
    kvw = n_kv * hd
    kv_width = 6 * kvw
    assert ts == 1 and hd == LANE and wb == min(WINDOW, past_len)
    mp = batch * t_len

    att_w_in_b, att_w_out_b = cast_bf16(att_w_in), cast_bf16(att_w_out)
    s5_w_glu_b = cast_bf16(s5_w_glu)
    cmp_w1_b = cast_bf16(cmp_w1)
    ffn_w_gu_b, ffn_w_down_b = cast_bf16(ffn_w_gate_up), cast_bf16(ffn_w_down)

    hp = x_prompt.reshape(mp, d)
    hs = x_sample.reshape(bd, d)
    kv_p, kv_s, win_p, win_s = [], [], [], []
    sr_p, si_p, sr_s, si_s = [], [], [], []
    pos_p = jnp.tile(jnp.arange(t_len, dtype=jnp.int32), batch)
    pos_s = jnp.full((bd,), past_len, jnp.int32)

    for i in range(depth):
        li = i // 2
        if i % 2 == 0:
            up = rmsnorm(hp, norm_mix[i], BF16)
            us = rmsnorm(hs, norm_mix[i], BF16)
            pe, w2 = cmp_pe[li], cmp_w2[li]
            cfg = dict(q_width=q_width, kv_width=kv_width, kvw=kvw, hd=hd)

            proj = in_proj(up, att_w_in_b, li, pos_p, **cfg)
            proj3 = proj.reshape(batch, t_len, -1)
            kv_p.append(proj3[:, :, q_width:q_width + 4 * kvw].reshape(batch, t_len, 4, n_kv, hd))
            w_keep = min(WINDOW, t_len)
            win_p.append(proj3[:, t_len - w_keep:, q_width + 4 * kvw:q_width + 6 * kvw]
                         .reshape(batch, w_keep, 2, n_kv, hd))
            n_cmp = t_len // CMP_STRIDE - 1
            ncp = -(-n_cmp // LANE) * LANE
            kvc_p = compress_proj(proj, pe, cmp_w1_b[li], w2, batch=batch, t_len=t_len, n_kv=n_kv, hd=hd,
                                  col0=q_width, ncp=ncp)
            gates_r = proj[:, q_width + kv_width:].reshape(mp, n_kv, 3 * gqa).transpose(1, 0, 2)
            o = nsa_prompt_attention(proj, gates_r, kvc_p[0], kvc_p[1], batch=batch, t_len=t_len, n_kv=n_kv,
                                     gqa=gqa, hd=hd, n_cmp=n_cmp)
            hp = out_proj(o, att_w_out_b, li, hp)

            proj_s = in_proj(us, att_w_in_b, li, pos_s, **cfg)
            rows_s = proj_s[:, q_width:q_width + kv_width].reshape(bd, 6, n_kv, hd)
            kv_s.append(rows_s[:, None, :4])
            win_s.append(jnp.concatenate([cache_win[li][:, ts:], rows_s[:, None, 4:]], axis=1))
            tk_len = past_len + ts
            assert past_len % CMP_STRIDE == 0 and ts < CMP_STRIDE
            n_cmp_s = tk_len // CMP_STRIDE - 1
            ncp_s = -(-n_cmp_s // LANE) * LANE
            cache4 = cache_nsa[li].reshape(n_pool, page, 4 * n_kv, hd)
            kvc = compress_paged(page_table, cache4, pe, cmp_w1_b[li], w2, n_kv=n_kv, hd=hd, ncp=ncp_s)
            q_s = proj_s[:, :q_width].reshape(bd, n_kv, gqa, hd)
            n_sel_s = -(-tk_len // SEL_BLOCK)
            k_sel_s = min(N_SELECT, n_sel_s)
            o_cmp, idx = nsa_dec_select(q_s, kvc[0], kvc[1], n_cmp=n_cmp_s, n_sel=n_sel_s, pos_q=past_len)
            new_rows = jnp.pad(rows_s.transpose(0, 2, 1, 3), ((0, 0), (0, 0), (0, SUBLANE - 6), (0, 0)))
            gates_s = proj_s[:, q_width + kv_width:].reshape(bd, n_kv, gqa, 3)
            o_s = nsa_dec_attend(page_table, idx[:, :, 0, :k_sel_s].reshape(-1), q_s, cache4, new_rows,
                                 cache_win[li].reshape(bd, wb, 2 * kvw), o_cmp, gates_s, k_sel=k_sel_s,
                                 past_len=past_len, pos_q=past_len)
            hs = out_proj(o_s.reshape(bd, q_width), att_w_out_b, li, hs)
        else:
            up = rmsnorm(hp, norm_mix[i], F32, time_major=(batch, t_len))
            us = rmsnorm(hs, norm_mix[i], F32)
            prm = s5_prepare(s5_lambda_re[li], s5_lambda_im[li], s5_log_dt[li], s5_b_re[li], s5_b_im[li],
                             s5_c_re[li], s5_c_im[li], s5_d[li])
            g_cnt, n_state = s5_lambda_re.shape[1:]

            zeros = jnp.zeros((batch, g_cnt, n_state), F32)
            z, hr, hi = s5_scan(up, zeros, zeros, prm, batch=batch, t_len=t_len)
            sr_p.append(hr)
            si_p.append(hi)
            hp = glu_proj(z, s5_w_glu_b, li, hp, time_major=(batch, t_len))

            z, hr, hi = s5_scan(us, state_s5_re[li].astype(F32), state_s5_im[li].astype(F32), prm,
                                batch=bd, t_len=ts)
            sr_s.append(hr)
            si_s.append(hi)
            hs = glu_proj(z, s5_w_glu_b, li, hs)

        hp = ffn(hp, rmsnorm(hp, norm_ffn[i], BF16), ffn_w_gu_b, ffn_w_down_b, i)
        hs = ffn(hs, rmsnorm(hs, norm_ffn[i], BF16), ffn_w_gu_b, ffn_w_down_b, i)

    y_p = rmsnorm(hp, norm_final, F32).reshape(batch, t_len, d)
    y_s = rmsnorm(hs, norm_final, F32).reshape(bd, ts, d)
    return (y_p, y_s, jnp.stack(kv_p), jnp.stack(kv_s), jnp.stack(win_p), jnp.stack(win_s),
            jnp.stack(sr_p), jnp.stack(si_p), jnp.stack(sr_s), jnp.stack(si_s))
```
